```python
import math
import jax
import jax.numpy as jnp
from jax import lax
import numpy as np

D_MODEL = 4096
BATCH = 1
SEQ = 8192
DEPTH = 1

DA_HEADS = 16
DA_HEAD_DIM = 128
DA_V_DIM = 2 * DA_HEAD_DIM
DA_WIDTH = DA_HEADS * DA_V_DIM
SG_WIDTH = 4096
SG_GROUPS = 16
SG_GROUP_DIM = SG_WIDTH // SG_GROUPS
CHUNK = 128
NUM_BUCKETS = 32
MAX_DISTANCE = 128
MEM_LEN = 256
XA_HEADS = 4
XA_HEAD_DIM = 128
XA_WIDTH = XA_HEADS * XA_HEAD_DIM
Q_BLOCK = 128
EPS = 1e-6
IN_SIZES = (DA_WIDTH, DA_WIDTH, DA_WIDTH, DA_WIDTH, SG_WIDTH, SG_WIDTH, SG_WIDTH, 2 * D_MODEL)
IN_WIDTH = 4 * DA_WIDTH + 3 * SG_WIDTH + 2 * D_MODEL

kernel_name = "hybrid_diffattn_chunk_sgu_block"


def _split_points(sizes):
    pts, acc = [], 0
    for s in sizes[:-1]:
        acc += s
        pts.append(acc)
    return pts


def rms_norm(x, g):
    xf = x.astype(jnp.float32)
    y = xf * lax.rsqrt(jnp.mean(xf * xf, axis=-1, keepdims=True) + EPS)
    return (y * g.astype(jnp.float32)).astype(x.dtype)


def layer_norm(x, g, b):
    xf = x.astype(jnp.float32)
    mu = jnp.mean(xf, axis=-1, keepdims=True)
    xc = xf - mu
    y = xc * lax.rsqrt(jnp.mean(xc * xc, axis=-1, keepdims=True) + EPS)
    return (y * g.astype(jnp.float32) + b.astype(jnp.float32)).astype(x.dtype)


def t5_bucket(rel):
    n = jnp.maximum(rel, 0)
    max_exact = NUM_BUCKETS // 2
    nf = jnp.maximum(n, 1).astype(jnp.float32)
    large = max_exact + (jnp.log(nf / max_exact) / math.log(MAX_DISTANCE / max_exact)
                         * (NUM_BUCKETS - max_exact)).astype(jnp.int32)
    large = jnp.minimum(large, NUM_BUCKETS - 1)
    return jnp.where(n < max_exact, n, large)


def diff_attention(q, k, v, positions, rel_bias, lam):
    B, S, H, _, Dh = q.shape
    nb = S // Q_BLOCK
    scale = Dh ** -0.5
    qb = q.reshape(B, nb, Q_BLOCK, H, 2, Dh).transpose(1, 0, 2, 3, 4, 5)
    pb = positions.reshape(B, nb, Q_BLOCK).transpose(1, 0, 2)

    def block(args):
        qi, pi = args
        s = jnp.einsum('bqhcd,bkhcd->bchqk', qi, k).astype(jnp.float32) * scale
        rel = pi[:, :, None] - positions[:, None, :]
        bias = rel_bias[t5_bucket(rel)]
        bias = bias.transpose(0, 3, 1, 2)[:, None].astype(jnp.float32)
        s = jnp.where((rel >= 0)[:, None, None], s + bias, -jnp.inf)
        p = jax.nn.softmax(s, axis=-1)
        a = p[:, 0] - lam * p[:, 1]
        return jnp.einsum('bhqk,bkhd->bqhd', a.astype(v.dtype), v)

    o = lax.map(block, (qb, pb))
    return o.transpose(1, 0, 2, 3, 4).reshape(B, S, H, v.shape[-1])


def chunk_spatial_gate(u, v, w_s, b_s, ln_g, ln_b):
    B, S, _ = v.shape
    v = layer_norm(v, ln_g, ln_b)
    vc = v.reshape(B, S // CHUNK, CHUNK, SG_GROUPS, SG_GROUP_DIM)
    mask = jnp.tril(jnp.ones((CHUNK, CHUNK), dtype=bool))
    w = jnp.where(mask[None], w_s, jnp.zeros_like(w_s))
    mixed = jnp.einsum('gts,bnsgd->bntgd', w, vc) + b_s.T[None, None, :, :, None]
    return u * mixed.reshape(B, S, SG_WIDTH)


def memory_cross_attention(h, mem, w_xq, w_xkv, w_xo):
    B, S, _ = h.shape
    q = (h @ w_xq).reshape(B, S, XA_HEADS, XA_HEAD_DIM)
    kv = (mem @ w_xkv).reshape(B, MEM_LEN, 2, XA_HEADS, XA_HEAD_DIM)
    k, v = kv[:, :, 0], kv[:, :, 1]
    s = jnp.einsum('bqhd,bkhd->bhqk', q, k).astype(jnp.float32) * (XA_HEAD_DIM ** -0.5)
    p = jax.nn.softmax(s, axis=-1).astype(v.dtype)
    o = jnp.einsum('bhqk,bkhd->bqhd', p, v).reshape(B, S, XA_WIDTH)
    return o @ w_xo


def setup_inputs(seed: int = 0) -> dict:
    key = jax.random.key(seed)
    ks = jax.random.split(key, 26)
    f = jnp.float32

    def nrm(k, shape, scale):
        return jax.random.normal(k, shape, f) * scale

    x = nrm(ks[0], (BATCH, SEQ, D_MODEL), 1.0)
    start = jax.random.randint(ks[1], (BATCH, 1), 0, 4096, dtype=jnp.int32)
    positions = start + jnp.arange(SEQ, dtype=jnp.int32)[None, :]
    mem = nrm(ks[2], (BATCH, MEM_LEN, D_MODEL), 1.0)
    return {
        "x": x,
        "positions": positions,
        "mem": mem,
        "w_in": nrm(ks[3], (DEPTH, D_MODEL, IN_WIDTH), D_MODEL ** -0.5),
        "b_gate": nrm(ks[4], (DEPTH, 2 * D_MODEL), 0.1),
        "norm1_g": 1.0 + nrm(ks[5], (DEPTH, D_MODEL), 0.02),
        "lam_q1": nrm(ks[6], (DEPTH, DA_HEAD_DIM), 0.1),
        "lam_k1": nrm(ks[7], (DEPTH, DA_HEAD_DIM), 0.1),
        "lam_q2": nrm(ks[8], (DEPTH, DA_HEAD_DIM), 0.1),
        "lam_k2": nrm(ks[9], (DEPTH, DA_HEAD_DIM), 0.1),
        "subln_g": 1.0 + nrm(ks[10], (DEPTH, DA_V_DIM), 0.02),
        "rel_bias": nrm(ks[11], (NUM_BUCKETS, DA_HEADS), 0.5),
        "sg_ln_g": 1.0 + nrm(ks[12], (DEPTH, SG_WIDTH), 0.02),
        "sg_ln_b": nrm(ks[13], (DEPTH, SG_WIDTH), 0.02),
        "sg_w": nrm(ks[14], (DEPTH, SG_GROUPS, CHUNK, CHUNK), CHUNK ** -0.5),
        "sg_b": 1.0 + nrm(ks[15], (DEPTH, SG_GROUPS, CHUNK), 0.1),
        "w_proj_a": nrm(ks[16], (DEPTH, DA_WIDTH, D_MODEL), DA_WIDTH ** -0.5),
        "w_proj_b": nrm(ks[17], (DEPTH, SG_WIDTH, D_MODEL), SG_WIDTH ** -0.5),
        "w_out": nrm(ks[18], (DEPTH, D_MODEL, D_MODEL), D_MODEL ** -0.5),
        "norm_x_g": 1.0 + nrm(ks[19], (DEPTH, D_MODEL), 0.02),
        "norm_mem_g": 1.0 + nrm(ks[20], (DEPTH, D_MODEL), 0.02),
        "w_xq": nrm(ks[21], (DEPTH, D_MODEL, XA_WIDTH), D_MODEL ** -0.5),
        "w_xkv": nrm(ks[22], (DEPTH, D_MODEL, 2 * XA_WIDTH), D_MODEL ** -0.5),
        "w_xo": nrm(ks[23], (DEPTH, XA_WIDTH, D_MODEL), XA_WIDTH ** -0.5),
        "final_g": 1.0 + nrm(ks[24], (D_MODEL,), 0.02),
    }


def reference(x, positions, mem, w_in, b_gate, norm1_g, lam_q1, lam_k1, lam_q2, lam_k2,
              subln_g, rel_bias, sg_ln_g, sg_ln_b, sg_w, sg_b, w_proj_a, w_proj_b, w_out,
              norm_x_g, norm_mem_g, w_xq, w_xkv, w_xo, final_g):
    B, S, _ = x.shape
    f32 = jnp.float32
    h = x
    for l in range(DEPTH):
        lam_init = 0.8 - 0.6 * math.exp(-0.3 * l)
        hn = rms_norm(h, norm1_g[l])
        proj = hn @ w_in[l]
        q_a, k_a, v_a, z_a, u_b, v_b, z_b, g_pre = jnp.split(proj, _split_points(IN_SIZES), axis=-1)
        lam = (jnp.exp(jnp.sum(lam_q1[l].astype(f32) * lam_k1[l].astype(f32)))
               - jnp.exp(jnp.sum(lam_q2[l].astype(f32) * lam_k2[l].astype(f32))) + lam_init)
        q_a = q_a.reshape(B, S, DA_HEADS, 2, DA_HEAD_DIM)
        k_a = k_a.reshape(B, S, DA_HEADS, 2, DA_HEAD_DIM)
        v_a = v_a.reshape(B, S, DA_HEADS, DA_V_DIM)
        o_a = diff_attention(q_a, k_a, v_a, positions, rel_bias, lam)
        o_a = (rms_norm(o_a, subln_g[l]) * (1.0 - lam_init)).reshape(B, S, DA_WIDTH)
        y_a = (o_a * jax.nn.silu(z_a)) @ w_proj_a[l]
        o_b = chunk_spatial_gate(jax.nn.gelu(u_b), jax.nn.gelu(v_b), sg_w[l], sg_b[l],
                                 sg_ln_g[l], sg_ln_b[l])
        y_b = (o_b * jax.nn.silu(z_b)) @ w_proj_b[l]
        g = jax.nn.sigmoid(g_pre + b_gate[l])
        g_a, g_b = g[..., :D_MODEL], g[..., D_MODEL:]
        h = h + (g_a * y_a + g_b * y_b) @ w_out[l]
        hn = rms_norm(h, norm_x_g[l])
        mn = rms_norm(mem, norm_mem_g[l])
        h = h + memory_cross_attention(hn, mn, w_xq[l], w_xkv[l], w_xo[l])
    return rms_norm(h, final_g)
```

```python
import functools
import math

import jax
import jax.numpy as jnp
from jax import lax
from jax.experimental import pallas as pl
from jax.experimental.pallas import tpu as pltpu

F32 = jnp.float32
BF16 = jnp.bfloat16

EPS = 1e-6
LANE = 128
VMEM_LIMIT = 56 * 1024 * 1024

DA_HEADS = 16
DA_HEAD_DIM = 128
DA_V_DIM = 2 * DA_HEAD_DIM
SG_GROUPS = 16
CHUNK = 128
NUM_BUCKETS = 32
MAX_DISTANCE = 128
XA_HEADS = 4
XA_HEAD_DIM = 128

ATT_T = 512
NEG = -1e30


def _cparams(sem):
    return pltpu.CompilerParams(dimension_semantics=sem, vmem_limit_bytes=VMEM_LIMIT)


def _sigmoid(x):
    return 1.0 / (1.0 + jnp.exp(-x))


def _gelu_tanh(x):
    c = math.sqrt(2.0 / math.pi)
    return 0.5 * x * (1.0 + jnp.tanh(c * (x + 0.044715 * (x * x * x))))


def _rmsnorm_kernel(x_ref, g_ref, o_ref):
    x = x_ref[...]
    ms = jnp.mean(x * x, axis=-1, keepdims=True)
    o_ref[...] = ((x * lax.rsqrt(ms + EPS)) * g_ref[...]).astype(o_ref.dtype)


def _rmsnorm(x, g, out_dtype, rows=256):
    m, d = x.shape
    return pl.pallas_call(
        _rmsnorm_kernel,
        grid=(m // rows,),
        in_specs=[pl.BlockSpec((rows, d), lambda i: (i, 0)),
                  pl.BlockSpec((1, d), lambda i: (0, 0))],
        out_specs=pl.BlockSpec((rows, d), lambda i: (i, 0)),
        out_shape=jax.ShapeDtypeStruct((m, d), out_dtype),
        compiler_params=_cparams(("parallel",)),
        name="rmsnorm",
    )(x, g.reshape(1, d))


def _proj_kernel(a_ref, w_ref, *rest, epilogue):
    o_ref = rest[-1]
    acc = jnp.dot(a_ref[...], w_ref[...], preferred_element_type=F32)
    if epilogue == "silu":
        acc = acc * _sigmoid(acc)
    elif epilogue == "gelu":
        acc = _gelu_tanh(acc)
    elif epilogue == "sigmoid_bias":
        acc = _sigmoid(acc + rest[0][...])
    elif epilogue == "residual":
        acc = rest[0][...] + acc
    o_ref[...] = acc.astype(o_ref.dtype)


def _proj(a, w, col0, ncols, epilogue, out_dtype, extra=None, tm=1024, tn=1024):
    m, k = a.shape
    assert col0 % tn == 0 and ncols % tn == 0 and m % tm == 0
    jb = col0 // tn
    in_specs = [pl.BlockSpec((tm, k), lambda i, j: (i, 0)),
                pl.BlockSpec((k, tn), lambda i, j: (0, j + jb))]
    args = [a, w]
    if epilogue == "sigmoid_bias":
        in_specs.append(pl.BlockSpec((1, tn), lambda i, j: (0, j)))
        args.append(extra)
    elif epilogue == "residual":
        in_specs.append(pl.BlockSpec((tm, tn), lambda i, j: (i, j)))
        args.append(extra)
    return pl.pallas_call(
        functools.partial(_proj_kernel, epilogue=epilogue),
        grid=(m // tm, ncols // tn),
        in_specs=in_specs,
        out_specs=pl.BlockSpec((tm, tn), lambda i, j: (i, j)),
        out_shape=jax.ShapeDtypeStruct((m, ncols), out_dtype),
        compiler_params=_cparams(("parallel", "parallel")),
        name="proj_" + epilogue,
    )(*args)


def _t5_bucket(n):
    max_exact = NUM_BUCKETS // 2
    nf = jnp.maximum(n, 1).astype(F32)
    large = max_exact + (jnp.log(nf / max_exact) / math.log(MAX_DISTANCE / max_exact)
                         * (NUM_BUCKETS - max_exact)).astype(jnp.int32)
    large = jnp.minimum(large, NUM_BUCKETS - 1)
    return jnp.where(n < max_exact, n, large)


def _attn_kernel(rb_ref, q_ref, k_ref, v_ref, zs_ref, lam_ref, g_ref, o_ref,
                 d0_ref, d1_ref, acc1_ref, acc2_ref, m1_ref, l1_ref, m2_ref, l2_ref,
                 *, lam_init):
    t = ATT_T
    dh = DA_HEAD_DIM
    h = pl.program_id(0)
    i = pl.program_id(1)
    scale = dh ** -0.5
    c2 = scale * math.log2(math.e)

    @pl.when(i == 0)
    def _build_bias():
        r = lax.broadcasted_iota(jnp.int32, (LANE, LANE), 0)
        c = lax.broadcasted_iota(jnp.int32, (LANE, LANE), 1)
        n0 = r - c
        b0 = _t5_bucket(jnp.maximum(n0, 0))
        b1 = _t5_bucket(n0 + LANE)
        last = rb_ref[h, NUM_BUCKETS - 1]
        p0 = jnp.zeros((LANE, LANE), F32)
        p1 = jnp.zeros((LANE, LANE), F32)
        for b in range(NUM_BUCKETS):
            val = (rb_ref[h, b] - last) * (1.0 / scale)
            p0 = jnp.where(b0 == b, val, p0)
            p1 = jnp.where(b1 == b, val, p1)
        p0 = jnp.where(n0 >= 0, p0, NEG)
        nblk = t // LANE
        zero = jnp.zeros((LANE, LANE), F32)
        neg = jnp.full((LANE, LANE), NEG, F32)
        for a in range(nblk):
            for b in range(nblk):
                blk = p0 if a == b else p1 if a == b + 1 else zero if a > b else neg
                d0_ref[a * LANE:(a + 1) * LANE, b * LANE:(b + 1) * LANE] = blk
                d1_ref[a * LANE:(a + 1) * LANE, b * LANE:(b + 1) * LANE] = (
                    p1 if (a == 0 and b == nblk - 1) else zero)

    acc1_ref[...] = jnp.zeros_like(acc1_ref)
    acc2_ref[...] = jnp.zeros_like(acc2_ref)
    m1_ref[...] = jnp.full_like(m1_ref, NEG)
    m2_ref[...] = jnp.full_like(m2_ref, NEG)
    l1_ref[...] = jnp.zeros_like(l1_ref)
    l2_ref[...] = jnp.zeros_like(l2_ref)

    def step(kb, bias_ref):
        off = pl.multiple_of(kb * t, t)
        k = k_ref[pl.ds(off, t), :]
        v = v_ref[pl.ds(off, t), :]
        for c, (acc_ref, m_ref, l_ref) in enumerate(
                ((acc1_ref, m1_ref, l1_ref), (acc2_ref, m2_ref, l2_ref))):
            q = q_ref[:, c * dh:(c + 1) * dh]
            s = lax.dot_general(q, k[:, c * dh:(c + 1) * dh], (((1,), (1,)), ((), ())),
                                preferred_element_type=F32)
            if bias_ref is not None:
                s = s + bias_ref[...]
            m_old = m_ref[...]
            m_new = jnp.maximum(m_old, jnp.max(s, axis=1, keepdims=True))
            alpha = jnp.exp2((m_old - m_new) * c2)
            p = jnp.exp2((s - m_new) * c2)
            l_ref[...] = alpha * l_ref[...] + jnp.sum(p, axis=1, keepdims=True)
            acc_ref[...] = alpha * acc_ref[...] + jnp.dot(
                p.astype(BF16), v, preferred_element_type=F32)
            m_ref[...] = m_new

    def far_step(kb, carry):
        step(kb, None)
        return carry

    lax.fori_loop(0, jnp.maximum(i - 1, 0), far_step, 0)

    @pl.when(i >= 1)
    def _below_diagonal():
        step(i - 1, d1_ref)

    step(i, d0_ref)

    lam_v = lam_ref[...]
    lam = (jnp.exp(jnp.sum(lam_v[0:1] * lam_v[1:2], axis=1, keepdims=True))
           - jnp.exp(jnp.sum(lam_v[2:3] * lam_v[3:4], axis=1, keepdims=True)) + lam_init)
    o = acc1_ref[...] / l1_ref[...] - lam * (acc2_ref[...] / l2_ref[...])
    ms = jnp.mean(o * o, axis=-1, keepdims=True)
    y = ((o * lax.rsqrt(ms + EPS)) * g_ref[...]) * (1.0 - lam_init)
    o_ref[...] = (y * zs_ref[...]).astype(o_ref.dtype)


def _diff_attention(qkv, zs, rel_bias_t, lam_vecs, subln_g, lam_init):
    s_len = qkv.shape[0]
    t = ATT_T
    hw = DA_V_DIM
    nh = DA_HEADS
    grid = (nh, s_len // t)
    return pl.pallas_call(
        functools.partial(_attn_kernel, lam_init=lam_init),
        grid=grid,
        in_specs=[
            pl.BlockSpec(memory_space=pltpu.SMEM),
            pl.BlockSpec((t, hw), lambda h, i: (i, h)),
            pl.BlockSpec((s_len, hw), lambda h, i: (0, nh + h)),
            pl.BlockSpec((s_len, hw), lambda h, i: (0, 2 * nh + h)),
            pl.BlockSpec((t, hw), lambda h, i: (i, h)),
            pl.BlockSpec((4, DA_HEAD_DIM), lambda h, i: (0, 0)),
            pl.BlockSpec((1, hw), lambda h, i: (0, 0)),
        ],
        out_specs=pl.BlockSpec((t, hw), lambda h, i: (i, h)),
        out_shape=jax.ShapeDtypeStruct((s_len, nh * hw), BF16),
        scratch_shapes=[
            pltpu.VMEM((t, t), F32), pltpu.VMEM((t, t), F32),
            pltpu.VMEM((t, hw), F32), pltpu.VMEM((t, hw), F32),
            pltpu.VMEM((t, 1), F32), pltpu.VMEM((t, 1), F32),
            pltpu.VMEM((t, 1), F32), pltpu.VMEM((t, 1), F32),
        ],
        compiler_params=_cparams(("arbitrary", "arbitrary")),
        name="diff_attention",
    )(rel_bias_t, qkv, qkv, qkv, zs, lam_vecs, subln_g.reshape(1, hw))


def _sgu_kernel(gv_ref, gu_ref, sz_ref, lng_ref, lnb_ref, w_ref, bt_ref, o_ref):
    v = gv_ref[...]
    mu = jnp.mean(v, axis=-1, keepdims=True)
    xc = v - mu
    var = jnp.mean(xc * xc, axis=-1, keepdims=True)
    vn = ((xc * lax.rsqrt(var + EPS)) * lng_ref[...] + lnb_ref[...]).astype(BF16)
    r = lax.broadcasted_iota(jnp.int32, (CHUNK, CHUNK), 0)
    c = lax.broadcasted_iota(jnp.int32, (CHUNK, CHUNK), 1)
    tril = r >= c
    gd = v.shape[1] // SG_GROUPS
    for g in range(SG_GROUPS):
        sl = slice(g * gd, (g + 1) * gd)
        wg = jnp.where(tril, w_ref[g], jnp.zeros((CHUNK, CHUNK), BF16))
        mixed = jnp.dot(wg, vn[:, sl], preferred_element_type=F32) + bt_ref[:, g:g + 1]
        o_ref[:, sl] = ((gu_ref[:, sl] * mixed) * sz_ref[:, sl]).astype(o_ref.dtype)


def _spatial_gate(gv, gu, sz, ln_g, ln_b, w_bf16, b_t):
    s_len, width = gv.shape
    row = pl.BlockSpec((CHUNK, width), lambda i: (i, 0))
    vec = pl.BlockSpec((1, width), lambda i: (0, 0))
    return pl.pallas_call(
        _sgu_kernel,
        grid=(s_len // CHUNK,),
        in_specs=[row, row, row, vec, vec,
                  pl.BlockSpec((SG_GROUPS, CHUNK, CHUNK), lambda i: (0, 0, 0)),
                  pl.BlockSpec((CHUNK, SG_GROUPS), lambda i: (0, 0))],
        out_specs=row,
        out_shape=jax.ShapeDtypeStruct((s_len, width), BF16),
        compiler_params=_cparams(("parallel",)),
        name="spatial_gate",
    )(gv, gu, sz, ln_g.reshape(1, width), ln_b.reshape(1, width), w_bf16, b_t)


def _merge_kernel(a_ref, b_ref, wa_ref, wb_ref, ga_ref, gb_ref, o_ref):
    ya = jnp.dot(a_ref[...], wa_ref[...], preferred_element_type=F32)
    yb = jnp.dot(b_ref[...], wb_ref[...], preferred_element_type=F32)
    o_ref[...] = (ga_ref[...] * ya + gb_ref[...] * yb).astype(o_ref.dtype)


def _merge(a, b, wa, wb, g, tm=512, tn=512):
    m, k = a.shape
    n = wa.shape[1]
    nb = n // tn
    return pl.pallas_call(
        _merge_kernel,
        grid=(m // tm, nb),
        in_specs=[pl.BlockSpec((tm, k), lambda i, j: (i, 0)),
                  pl.BlockSpec((tm, k), lambda i, j: (i, 0)),
                  pl.BlockSpec((k, tn), lambda i, j: (0, j)),
                  pl.BlockSpec((k, tn), lambda i, j: (0, j)),
                  pl.BlockSpec((tm, tn), lambda i, j: (i, j)),
                  pl.BlockSpec((tm, tn), lambda i, j: (i, j + nb))],
        out_specs=pl.BlockSpec((tm, tn), lambda i, j: (i, j)),
        out_shape=jax.ShapeDtypeStruct((m, n), BF16),
        compiler_params=_cparams(("parallel", "parallel")),
        name="merge",
    )(a, b, wa, wb, g, g)


def _mem_kv_kernel(mem_ref, g_ref, w_ref, o_ref):
    x = mem_ref[...]
    ms = jnp.mean(x * x, axis=-1, keepdims=True)
    mn = ((x * lax.rsqrt(ms + EPS)) * g_ref[...]).astype(BF16)
    o_ref[...] = jnp.dot(mn, w_ref[...], preferred_element_type=F32).astype(o_ref.dtype)


def _mem_kv(mem, g, w_xkv):
    ml, d = mem.shape
    n = w_xkv.shape[1]
    return pl.pallas_call(
        _mem_kv_kernel,
        grid=(1,),
        in_specs=[pl.BlockSpec((ml, d), lambda i: (0, 0)),
                  pl.BlockSpec((1, d), lambda i: (0, 0)),
                  pl.BlockSpec((d, n), lambda i: (0, 0))],
        out_specs=pl.BlockSpec((ml, n), lambda i: (0, 0)),
        out_shape=jax.ShapeDtypeStruct((ml, n), BF16),
        compiler_params=_cparams(("arbitrary",)),
        name="mem_kv",
    )(mem, g.reshape(1, d), w_xkv)


def _xattn_kernel(h_ref, gx_ref, wq_ref, kv_ref, wo_ref, gf_ref, o_ref):
    hd = XA_HEAD_DIM
    xw = XA_HEADS * hd
    h = h_ref[...]
    ms = jnp.mean(h * h, axis=-1, keepdims=True)
    hn = ((h * lax.rsqrt(ms + EPS)) * gx_ref[...]).astype(BF16)
    q = jnp.dot(hn, wq_ref[...], preferred_element_type=F32).astype(BF16)
    outs = []
    for a in range(XA_HEADS):
        k = kv_ref[:, a * hd:(a + 1) * hd]
        v = kv_ref[:, xw + a * hd:xw + (a + 1) * hd]
        s = lax.dot_general(q[:, a * hd:(a + 1) * hd], k, (((1,), (1,)), ((), ())),
                            preferred_element_type=F32) * (hd ** -0.5)
        p = jnp.exp(s - jnp.max(s, axis=-1, keepdims=True))
        p = p / jnp.sum(p, axis=-1, keepdims=True)
        outs.append(jnp.dot(p.astype(BF16), v, preferred_element_type=F32))
    o = jnp.concatenate(outs, axis=-1).astype(BF16)
    h2 = h + jnp.dot(o, wo_ref[...], preferred_element_type=F32)
    ms2 = jnp.mean(h2 * h2, axis=-1, keepdims=True)
    o_ref[...] = (h2 * lax.rsqrt(ms2 + EPS)) * gf_ref[...]


def _xattn_final(h, gx, wq, kv, wo, gf, tm=256):
    m, d = h.shape
    row = pl.BlockSpec((tm, d), lambda i: (i, 0))
    vec = pl.BlockSpec((1, d), lambda i: (0, 0))
    full = lambda arr: pl.BlockSpec(arr.shape, lambda i: (0, 0))
    return pl.pallas_call(
        _xattn_kernel,
        grid=(m // tm,),
        in_specs=[row, vec, full(wq), full(kv), full(wo), vec],
        out_specs=row,
        out_shape=jax.ShapeDtypeStruct((m, d), F32),
        compiler_params=_cparams(("parallel",)),
        name="xattn_final",
    )(h, gx.reshape(1, d), wq, kv, wo, gf.reshape(1, d))


def kernel(x, positions, mem, w_in, b_gate, norm1_g, lam_q1, lam_k1, lam_q2, lam_k2,
           subln_g, rel_bias, sg_ln_g, sg_ln_b, sg_w, sg_b, w_proj_a, w_proj_b, w_out,
           norm_x_g, norm_mem_g, w_xq, w_xkv, w_xo, final_g):
    bsz, s_len, d = x.shape
    assert bsz == 1 and positions.shape == (bsz, s_len)
    depth = w_in.shape[0]
    w = DA_HEADS * DA_V_DIM
    h = x.reshape(s_len, d)
    rel_bias_t = rel_bias.T
    for l in range(depth):
        lam_init = 0.8 - 0.6 * math.exp(-0.3 * l)
        w_in_b = w_in[l].astype(BF16)
        hn = _rmsnorm(h, norm1_g[l], BF16)
        qkv = _proj(hn, w_in_b, 0, 3 * w, "cast", BF16)
        zs_a = _proj(hn, w_in_b, 3 * w, w, "silu", F32)
        gu = _proj(hn, w_in_b, 4 * w, w, "gelu", F32)
        gv = _proj(hn, w_in_b, 5 * w, w, "gelu", F32)
        zs_b = _proj(hn, w_in_b, 6 * w, w, "silu", F32)
        gates = _proj(hn, w_in_b, 7 * w, 2 * d, "sigmoid_bias", F32,
                      extra=b_gate[l].reshape(1, 2 * d))
        lam_vecs = jnp.stack([lam_q1[l], lam_k1[l], lam_q2[l], lam_k2[l]]).astype(F32)
        a_in = _diff_attention(qkv, zs_a, rel_bias_t, lam_vecs, subln_g[l], lam_init)
        b_in = _spatial_gate(gv, gu, zs_b, sg_ln_g[l], sg_ln_b[l],
                             sg_w[l].astype(BF16), sg_b[l].T)
        merged = _merge(a_in, b_in, w_proj_a[l].astype(BF16), w_proj_b[l].astype(BF16), gates)
        h = _proj(merged, w_out[l].astype(BF16), 0, d, "residual", F32, extra=h, tn=512)
        kv = _mem_kv(mem.reshape(-1, d), norm_mem_g[l], w_xkv[l].astype(BF16))
        last = l == depth - 1
        assert last, "stacked layers would need an un-normalised hand-off"
        h = _xattn_final(h, norm_x_g[l], w_xq[l].astype(BF16), kv, w_xo[l].astype(BF16), final_g)
    return h.reshape(bsz, s_len, d)
```

```python
import functools
import math

import jax
import jax.numpy as jnp
from jax import lax
from jax.experimental import pallas as pl
from jax.experimental.pallas import tpu as pltpu

F32 = jnp.float32
BF16 = jnp.bfloat16

EPS = 1e-6
LANE = 128
VMEM_LIMIT = 56 * 1024 * 1024

DA_HEADS = 16
DA_HEAD_DIM = 128
DA_V_DIM = 2 * DA_HEAD_DIM
SG_GROUPS = 16
CHUNK = 128
NUM_BUCKETS = 32
MAX_DISTANCE = 128
XA_HEADS = 4
XA_HEAD_DIM = 128

ATT_T = 512
NEG = -1e30


def _cparams(sem):
    return pltpu.CompilerParams(dimension_semantics=sem, vmem_limit_bytes=VMEM_LIMIT)


def _sigmoid(x):
    return 1.0 / (1.0 + jnp.exp(-x))


def _gelu_tanh(x):
    c = math.sqrt(2.0 / math.pi)
    return 0.5 * x * (1.0 + jnp.tanh(c * (x + 0.044715 * (x * x * x))))


def _rmsnorm_kernel(x_ref, g_ref, o_ref):
    x = x_ref[...]
    ms = jnp.mean(x * x, axis=-1, keepdims=True)
    o_ref[...] = ((x * lax.rsqrt(ms + EPS)) * g_ref[...]).astype(o_ref.dtype)


def _rmsnorm(x, g, out_dtype, rows=256):
    m, d = x.shape
    return pl.pallas_call(
        _rmsnorm_kernel,
        grid=(m // rows,),
        in_specs=[pl.BlockSpec((rows, d), lambda i: (i, 0)),
                  pl.BlockSpec((1, d), lambda i: (0, 0))],
        out_specs=pl.BlockSpec((rows, d), lambda i: (i, 0)),
        out_shape=jax.ShapeDtypeStruct((m, d), out_dtype),
        compiler_params=_cparams(("parallel",)),
        name="rmsnorm",
    )(x, g.reshape(1, d))


def _proj_kernel(a_ref, w_ref, *rest, epilogue):
    o_ref = rest[-1]
    acc = jnp.dot(a_ref[...], w_ref[...], preferred_element_type=F32)
    if epilogue == "silu":
        acc = acc * _sigmoid(acc)
    elif epilogue == "gelu":
        acc = _gelu_tanh(acc)
    elif epilogue == "sigmoid_bias":
        acc = _sigmoid(acc + rest[0][...])
    elif epilogue == "residual":
        acc = rest[0][...] + acc
    o_ref[...] = acc.astype(o_ref.dtype)


def _proj(a, w, col0, ncols, epilogue, out_dtype, extra=None, tm=1024, tn=1024):
    m, k = a.shape
    assert col0 % tn == 0 and ncols % tn == 0 and m % tm == 0
    jb = col0 // tn
    in_specs = [pl.BlockSpec((tm, k), lambda i, j: (i, 0)),
                pl.BlockSpec((k, tn), lambda i, j: (0, j + jb))]
    args = [a, w]
    if epilogue == "sigmoid_bias":
        in_specs.append(pl.BlockSpec((1, tn), lambda i, j: (0, j)))
        args.append(extra)
    elif epilogue == "residual":
        in_specs.append(pl.BlockSpec((tm, tn), lambda i, j: (i, j)))
        args.append(extra)
    return pl.pallas_call(
        functools.partial(_proj_kernel, epilogue=epilogue),
        grid=(m // tm, ncols // tn),
        in_specs=in_specs,
        out_specs=pl.BlockSpec((tm, tn), lambda i, j: (i, j)),
        out_shape=jax.ShapeDtypeStruct((m, ncols), out_dtype),
        compiler_params=_cparams(("parallel", "parallel")),
        name="proj_" + epilogue,
    )(*args)


def _t5_bucket(n):
    max_exact = NUM_BUCKETS // 2
    nf = jnp.maximum(n, 1).astype(F32)
    large = max_exact + (jnp.log(nf / max_exact) / math.log(MAX_DISTANCE / max_exact)
                         * (NUM_BUCKETS - max_exact)).astype(jnp.int32)
    large = jnp.minimum(large, NUM_BUCKETS - 1)
    return jnp.where(n < max_exact, n, large)


def _attn_kernel(rb_ref, q_ref, k_ref, v_ref, zs_ref, lam_ref, g_ref, o_ref,
                 dn_ref, acc_ref, m_ref, l_ref,
                 *, lam_init):
    t = ATT_T
    dh = DA_HEAD_DIM
    h = pl.program_id(0)
    i = pl.program_id(1)
    scale = dh ** -0.5
    c2 = scale * math.log2(math.e)

    @pl.when(i == 0)
    def _build_bias():
        r = lax.broadcasted_iota(jnp.int32, (LANE, LANE), 0)
        c = lax.broadcasted_iota(jnp.int32, (LANE, LANE), 1)
        n0 = r - c
        b0 = _t5_bucket(jnp.maximum(n0, 0))
        b1 = _t5_bucket(n0 + LANE)
        last = rb_ref[h, NUM_BUCKETS - 1]
        p0 = jnp.zeros((LANE, LANE), F32)
        p1 = jnp.zeros((LANE, LANE), F32)
        for b in range(NUM_BUCKETS):
            val = (rb_ref[h, b] - last) * (1.0 / scale)
            p0 = jnp.where(b0 == b, val, p0)
            p1 = jnp.where(b1 == b, val, p1)
        p0 = jnp.where(n0 >= 0, p0, NEG)
        nblk = t // LANE
        zero = jnp.zeros((LANE, LANE), F32)
        neg = jnp.full((LANE, LANE), NEG, F32)
        for a in range(nblk):
            for b in range(nblk):
                dn_ref[a * LANE:(a + 1) * LANE, b * LANE:(b + 1) * LANE] = (
                    p1 if (a == 0 and b == nblk - 1) else zero)
                blk = p0 if a == b else p1 if a == b + 1 else zero if a > b else neg
                dn_ref[a * LANE:(a + 1) * LANE, t + b * LANE:t + (b + 1) * LANE] = blk

    acc_ref[...] = jnp.zeros_like(acc_ref)
    m_ref[...] = jnp.full_like(m_ref, NEG)
    l_ref[...] = jnp.zeros_like(l_ref)

    def step(kb, width, bias=None):
        tk = width * t
        off = pl.multiple_of(kb * t, t)
        k = k_ref[pl.ds(off, tk), :]
        v = v_ref[pl.ds(off, tk), :]
        ss = []
        for c in range(2):
            q = q_ref[:, c * dh:(c + 1) * dh]
            s = lax.dot_general(q, k[:, c * dh:(c + 1) * dh], (((1,), (1,)), ((), ())),
                                preferred_element_type=F32)
            ss.append(s if bias is None else s + bias)
        m_out, l_out, alphas, probs = [], [], [], []
        for c in range(2):
            rows = slice(c * t, (c + 1) * t)
            m_old = m_ref[rows, :]
            m_new = jnp.maximum(m_old, jnp.max(ss[c], axis=1, keepdims=True))
            alpha = jnp.exp2((m_old - m_new) * c2)
            ps = [jnp.exp2((ss[c][:, j * LANE:(j + 1) * LANE] - m_new) * c2)
                  for j in range(tk // LANE)]
            l_out.append(alpha * l_ref[rows, :] + functools.reduce(lambda a, b: a + b, ps))
            probs.append(jnp.concatenate(ps, axis=1).astype(BF16))
            alphas.append(alpha)
            m_out.append(m_new)
        acc_out = []
        for c in range(2):
            rows = slice(c * t, (c + 1) * t)
            pv = jnp.dot(probs[c], v, preferred_element_type=F32)
            alpha_w = jnp.concatenate([alphas[c]] * (acc_ref.shape[1] // LANE), axis=1)
            acc_out.append(alpha_w * acc_ref[rows, :] + pv)
        m_ref[...] = jnp.concatenate(m_out, axis=0)
        l_ref[...] = jnp.concatenate(l_out, axis=0)
        acc_ref[...] = jnp.concatenate(acc_out, axis=0)

    n_far = jnp.maximum(i - 1, 0)

    def far_quad(j, carry):
        step(4 * j, 4)
        return carry

    lax.fori_loop(0, n_far // 4, far_quad, 0)

    @pl.when(n_far % 4 >= 2)
    def _far_pair():
        step((n_far // 4) * 4, 2)

    @pl.when(n_far % 2 == 1)
    def _far_single():
        step(n_far - 1, 1)

    @pl.when(i >= 1)
    def _below_and_on_diagonal():
        step(i - 1, 2, dn_ref[...])

    @pl.when(i == 0)
    def _diagonal_only():
        step(0, 1, dn_ref[:, t:2 * t])

    lam_v = lam_ref[...]
    lam = (jnp.exp(jnp.sum(lam_v[0:1] * lam_v[1:2], axis=1, keepdims=True))
           - jnp.exp(jnp.sum(lam_v[2:3] * lam_v[3:4], axis=1, keepdims=True)) + lam_init)
    l_sum = jnp.sum(l_ref[...], axis=1, keepdims=True)
    o = acc_ref[0:t, :] / l_sum[0:t] - lam * (acc_ref[t:2 * t, :] / l_sum[t:2 * t])
    ms = jnp.mean(o * o, axis=-1, keepdims=True)
    y = ((o * lax.rsqrt(ms + EPS)) * g_ref[...]) * (1.0 - lam_init)
    o_ref[...] = (y * zs_ref[...]).astype(o_ref.dtype)


def _diff_attention(qkv, zs, rel_bias_t, lam_vecs, subln_g, lam_init):
    s_len = qkv.shape[0]
    t = ATT_T
    hw = DA_V_DIM
    nh = DA_HEADS
    grid = (nh, s_len // t)
    return pl.pallas_call(
        functools.partial(_attn_kernel, lam_init=lam_init),
        grid=grid,
        in_specs=[
            pl.BlockSpec(memory_space=pltpu.SMEM),
            pl.BlockSpec((t, hw), lambda h, i: (i, h)),
            pl.BlockSpec((s_len, hw), lambda h, i: (0, nh + h)),
            pl.BlockSpec((s_len, hw), lambda h, i: (0, 2 * nh + h)),
            pl.BlockSpec((t, hw), lambda h, i: (i, h)),
            pl.BlockSpec((4, DA_HEAD_DIM), lambda h, i: (0, 0)),
            pl.BlockSpec((1, hw), lambda h, i: (0, 0)),
        ],
        out_specs=pl.BlockSpec((t, hw), lambda h, i: (i, h)),
        out_shape=jax.ShapeDtypeStruct((s_len, nh * hw), BF16),
        scratch_shapes=[
            pltpu.VMEM((t, 2 * t), F32),
            pltpu.VMEM((2 * t, hw), F32),
            pltpu.VMEM((2 * t, LANE), F32), pltpu.VMEM((2 * t, LANE), F32),
        ],
        compiler_params=_cparams(("arbitrary", "arbitrary")),
        name="diff_attention",
    )(rel_bias_t, qkv, qkv, qkv, zs, lam_vecs, subln_g.reshape(1, hw))


def _sgu_kernel(gv_ref, gu_ref, sz_ref, lng_ref, lnb_ref, w_ref, bt_ref, o_ref):
    v = gv_ref[...]
    mu = jnp.mean(v, axis=-1, keepdims=True)
    xc = v - mu
    var = jnp.mean(xc * xc, axis=-1, keepdims=True)
    vn = ((xc * lax.rsqrt(var + EPS)) * lng_ref[...] + lnb_ref[...]).astype(BF16)
    r = lax.broadcasted_iota(jnp.int32, (CHUNK, CHUNK), 0)
    c = lax.broadcasted_iota(jnp.int32, (CHUNK, CHUNK), 1)
    tril = r >= c
    gd = v.shape[1] // SG_GROUPS
    for g in range(SG_GROUPS):
        sl = slice(g * gd, (g + 1) * gd)
        wg = jnp.where(tril, w_ref[g], jnp.zeros((CHUNK, CHUNK), BF16))
        mixed = jnp.dot(wg, vn[:, sl], preferred_element_type=F32) + bt_ref[:, g:g + 1]
        o_ref[:, sl] = ((gu_ref[:, sl] * mixed) * sz_ref[:, sl]).astype(o_ref.dtype)


def _spatial_gate(gv, gu, sz, ln_g, ln_b, w_bf16, b_t):
    s_len, width = gv.shape
    row = pl.BlockSpec((CHUNK, width), lambda i: (i, 0))
    vec = pl.BlockSpec((1, width), lambda i: (0, 0))
    return pl.pallas_call(
        _sgu_kernel,
        grid=(s_len // CHUNK,),
        in_specs=[row, row, row, vec, vec,
                  pl.BlockSpec((SG_GROUPS, CHUNK, CHUNK), lambda i: (0, 0, 0)),
                  pl.BlockSpec((CHUNK, SG_GROUPS), lambda i: (0, 0))],
        out_specs=row,
        out_shape=jax.ShapeDtypeStruct((s_len, width), BF16),
        compiler_params=_cparams(("parallel",)),
        name="spatial_gate",
    )(gv, gu, sz, ln_g.reshape(1, width), ln_b.reshape(1, width), w_bf16, b_t)


def _merge_kernel(a_ref, b_ref, wa_ref, wb_ref, ga_ref, gb_ref, o_ref):
    ya = jnp.dot(a_ref[...], wa_ref[...], preferred_element_type=F32)
    yb = jnp.dot(b_ref[...], wb_ref[...], preferred_element_type=F32)
    o_ref[...] = (ga_ref[...] * ya + gb_ref[...] * yb).astype(o_ref.dtype)


def _merge(a, b, wa, wb, g, tm=512, tn=512):
    m, k = a.shape
    n = wa.shape[1]
    nb = n // tn
    return pl.pallas_call(
        _merge_kernel,
        grid=(m // tm, nb),
        in_specs=[pl.BlockSpec((tm, k), lambda i, j: (i, 0)),
                  pl.BlockSpec((tm, k), lambda i, j: (i, 0)),
                  pl.BlockSpec((k, tn), lambda i, j: (0, j)),
                  pl.BlockSpec((k, tn), lambda i, j: (0, j)),
                  pl.BlockSpec((tm, tn), lambda i, j: (i, j)),
                  pl.BlockSpec((tm, tn), lambda i, j: (i, j + nb))],
        out_specs=pl.BlockSpec((tm, tn), lambda i, j: (i, j)),
        out_shape=jax.ShapeDtypeStruct((m, n), BF16),
        compiler_params=_cparams(("parallel", "parallel")),
        name="merge",
    )(a, b, wa, wb, g, g)


def _mem_kv_kernel(mem_ref, g_ref, w_ref, o_ref):
    x = mem_ref[...]
    ms = jnp.mean(x * x, axis=-1, keepdims=True)
    mn = ((x * lax.rsqrt(ms + EPS)) * g_ref[...]).astype(BF16)
    o_ref[...] = jnp.dot(mn, w_ref[...], preferred_element_type=F32).astype(o_ref.dtype)


def _mem_kv(mem, g, w_xkv):
    ml, d = mem.shape
    n = w_xkv.shape[1]
    return pl.pallas_call(
        _mem_kv_kernel,
        grid=(1,),
        in_specs=[pl.BlockSpec((ml, d), lambda i: (0, 0)),
                  pl.BlockSpec((1, d), lambda i: (0, 0)),
                  pl.BlockSpec((d, n), lambda i: (0, 0))],
        out_specs=pl.BlockSpec((ml, n), lambda i: (0, 0)),
        out_shape=jax.ShapeDtypeStruct((ml, n), BF16),
        compiler_params=_cparams(("arbitrary",)),
        name="mem_kv",
    )(mem, g.reshape(1, d), w_xkv)


def _xattn_kernel(h_ref, gx_ref, wq_ref, kv_ref, wo_ref, gf_ref, o_ref):
    hd = XA_HEAD_DIM
    xw = XA_HEADS * hd
    h = h_ref[...]
    ms = jnp.mean(h * h, axis=-1, keepdims=True)
    hn = ((h * lax.rsqrt(ms + EPS)) * gx_ref[...]).astype(BF16)
    q = jnp.dot(hn, wq_ref[...], preferred_element_type=F32).astype(BF16)
    outs = []
    for a in range(XA_HEADS):
        k = kv_ref[:, a * hd:(a + 1) * hd]
        v = kv_ref[:, xw + a * hd:xw + (a + 1) * hd]
        s = lax.dot_general(q[:, a * hd:(a + 1) * hd], k, (((1,), (1,)), ((), ())),
                            preferred_element_type=F32) * (hd ** -0.5)
        p = jnp.exp(s - jnp.max(s, axis=-1, keepdims=True))
        p = p / jnp.sum(p, axis=-1, keepdims=True)
        outs.append(jnp.dot(p.astype(BF16), v, preferred_element_type=F32))
    o = jnp.concatenate(outs, axis=-1).astype(BF16)
    h2 = h + jnp.dot(o, wo_ref[...], preferred_element_type=F32)
    ms2 = jnp.mean(h2 * h2, axis=-1, keepdims=True)
    o_ref[...] = (h2 * lax.rsqrt(ms2 + EPS)) * gf_ref[...]


def _xattn_final(h, gx, wq, kv, wo, gf, tm=256):
    m, d = h.shape
    row = pl.BlockSpec((tm, d), lambda i: (i, 0))
    vec = pl.BlockSpec((1, d), lambda i: (0, 0))
    full = lambda arr: pl.BlockSpec(arr.shape, lambda i: (0, 0))
    return pl.pallas_call(
        _xattn_kernel,
        grid=(m // tm,),
        in_specs=[row, vec, full(wq), full(kv), full(wo), vec],
        out_specs=row,
        out_shape=jax.ShapeDtypeStruct((m, d), F32),
        compiler_params=_cparams(("parallel",)),
        name="xattn_final",
    )(h, gx.reshape(1, d), wq, kv, wo, gf.reshape(1, d))


def kernel(x, positions, mem, w_in, b_gate, norm1_g, lam_q1, lam_k1, lam_q2, lam_k2,
           subln_g, rel_bias, sg_ln_g, sg_ln_b, sg_w, sg_b, w_proj_a, w_proj_b, w_out,
           norm_x_g, norm_mem_g, w_xq, w_xkv, w_xo, final_g):
    bsz, s_len, d = x.shape
    assert bsz == 1 and positions.shape == (bsz, s_len)
    depth = w_in.shape[0]
    w = DA_HEADS * DA_V_DIM
    h = x.reshape(s_len, d)
    rel_bias_t = rel_bias.T
    for l in range(depth):
        lam_init = 0.8 - 0.6 * math.exp(-0.3 * l)
        w_in_b = w_in[l].astype(BF16)
        hn = _rmsnorm(h, norm1_g[l], BF16)
        qkv = _proj(hn, w_in_b, 0, 3 * w, "cast", BF16)
        zs_a = _proj(hn, w_in_b, 3 * w, w, "silu", F32)
        gu = _proj(hn, w_in_b, 4 * w, w, "gelu", F32)
        gv = _proj(hn, w_in_b, 5 * w, w, "gelu", F32)
        zs_b = _proj(hn, w_in_b, 6 * w, w, "silu", F32)
        gates = _proj(hn, w_in_b, 7 * w, 2 * d, "sigmoid_bias", F32,
                      extra=b_gate[l].reshape(1, 2 * d))
        lam_vecs = jnp.stack([lam_q1[l], lam_k1[l], lam_q2[l], lam_k2[l]]).astype(F32)
        a_in = _diff_attention(qkv, zs_a, rel_bias_t, lam_vecs, subln_g[l], lam_init)
        b_in = _spatial_gate(gv, gu, zs_b, sg_ln_g[l], sg_ln_b[l],
                             sg_w[l].astype(BF16), sg_b[l].T)
        merged = _merge(a_in, b_in, w_proj_a[l].astype(BF16), w_proj_b[l].astype(BF16), gates)
        h = _proj(merged, w_out[l].astype(BF16), 0, d, "residual", F32, extra=h, tn=512)
        kv = _mem_kv(mem.reshape(-1, d), norm_mem_g[l], w_xkv[l].astype(BF16))
        last = l == depth - 1
        assert last, "stacked layers would need an un-normalised hand-off"
        h = _xattn_final(h, norm_x_g[l], w_xq[l].astype(BF16), kv, w_xo[l].astype(BF16), final_g)
    return h.reshape(bsz, s_len, d)
```

```python
import functools
import math

import jax
import jax.numpy as jnp
from jax import lax
from jax.experimental import pallas as pl
from jax.experimental.pallas import tpu as pltpu

F32 = jnp.float32
BF16 = jnp.bfloat16

EPS = 1e-6
LANE = 128
VMEM_LIMIT = 56 * 1024 * 1024

DA_HEADS = 16
DA_HEAD_DIM = 128
DA_V_DIM = 2 * DA_HEAD_DIM
SG_GROUPS = 16
CHUNK = 128
NUM_BUCKETS = 32
MAX_DISTANCE = 128
XA_HEADS = 4
XA_HEAD_DIM = 128

ATT_T = 512
SOFTMAX_C2 = DA_HEAD_DIM ** -0.5 * math.log2(math.e)
NEG = -1e30


def _cparams(sem):
    return pltpu.CompilerParams(dimension_semantics=sem, vmem_limit_bytes=VMEM_LIMIT)


def _sigmoid(x):
    return 1.0 / (1.0 + jnp.exp(-x))


def _gelu_tanh(x):
    c = math.sqrt(2.0 / math.pi)
    return 0.5 * x * (1.0 + jnp.tanh(c * (x + 0.044715 * (x * x * x))))


def _rmsnorm_kernel(x_ref, g_ref, o_ref):
    x = x_ref[...]
    ms = jnp.mean(x * x, axis=-1, keepdims=True)
    o_ref[...] = ((x * lax.rsqrt(ms + EPS)) * g_ref[...]).astype(o_ref.dtype)


def _rmsnorm(x, g, out_dtype, rows=512):
    m, d = x.shape
    return pl.pallas_call(
        _rmsnorm_kernel,
        grid=(m // rows,),
        in_specs=[pl.BlockSpec((rows, d), lambda i: (i, 0)),
                  pl.BlockSpec((1, d), lambda i: (0, 0))],
        out_specs=pl.BlockSpec((rows, d), lambda i: (i, 0)),
        out_shape=jax.ShapeDtypeStruct((m, d), out_dtype),
        compiler_params=_cparams(("parallel",)),
        name="rmsnorm",
    )(x, g.reshape(1, d))


def _epilogue(kind, ys, extras, col_tile):
    if kind == "cast":
        return ys[0]
    if kind == "qkv":
        q_tiles = DA_HEADS * DA_V_DIM // ys[0].shape[1]
        return ys[0] * jnp.where(col_tile < q_tiles, SOFTMAX_C2, 1.0)
    if kind == "silu":
        return ys[0] * _sigmoid(ys[0])
    if kind == "gelu":
        return _gelu_tanh(ys[0])
    if kind == "sigmoid_bias":
        return _sigmoid(ys[0] + extras[0][...])
    if kind == "residual":
        return extras[0][...] + ys[0]
    if kind == "gated_sum":
        return extras[0][...] * ys[0] + extras[1][...] * ys[1]
    raise ValueError(kind)


def _wproj_kernel(*refs, n_w, n_extra, epilogue, kc, col_blk0):
    a_refs = refs[:n_w]
    w_refs = refs[n_w:2 * n_w]
    extras = refs[2 * n_w:2 * n_w + n_extra]
    o_ref = refs[2 * n_w + n_extra]
    wb_refs = refs[2 * n_w + n_extra + 1:3 * n_w + n_extra + 1]
    st_refs = refs[3 * n_w + n_extra + 1:4 * n_w + n_extra + 1]
    sem = refs[-1]
    j = pl.program_id(0)
    i = pl.program_id(1)
    nj = pl.num_programs(0)
    ni = pl.num_programs(1)
    tn = o_ref.shape[1]

    def chunk_copy(w, jt, c, slot):
        return pltpu.make_async_copy(
            w_refs[w].at[pl.ds(c * kc, kc), pl.ds((col_blk0 + jt) * tn, tn)],
            st_refs[w].at[slot], sem.at[w, slot])

    def cast_chunk(w, c, slot, wslot):
        row0 = c * kc if isinstance(c, int) else pl.multiple_of(c * kc, kc)
        wb_refs[w][wslot, pl.ds(row0, kc), :] = st_refs[w][slot].astype(BF16)

    n_chunks = w_refs[0].shape[0] // kc

    @pl.when((j == 0) & (i == 0))
    def _first_tile():
        for w in range(n_w):
            chunk_copy(w, 0, 0, 0).start()
        for c in range(n_chunks):
            for w in range(n_w):
                if c + 1 < n_chunks:
                    chunk_copy(w, 0, c + 1, (c + 1) % 2).start()
                chunk_copy(w, 0, c, c % 2).wait()
                cast_chunk(w, c, c % 2, 0)
        for w in range(n_w):
            chunk_copy(w, 1 % nj, 0, 0).start()

    slot = i % 2
    cur = j % 2
    nxt = 1 - cur
    next_tile = (j + 1) % nj

    @pl.when(jnp.logical_not((j == nj - 1) & (i == ni - 1)))
    def _prefetch():
        wrap = i == ni - 1
        jt = jnp.where(wrap, (j + 2) % nj, next_tile)
        c = jnp.where(wrap, 0, i + 1)
        for w in range(n_w):
            chunk_copy(w, jt, c, 1 - slot).start()

    for w in range(n_w):
        chunk_copy(w, next_tile, i, slot).wait()
        cast_chunk(w, i, slot, nxt)

    ys = [jnp.dot(a_refs[w][...], wb_refs[w][cur], preferred_element_type=F32)
          for w in range(n_w)]
    o_ref[...] = _epilogue(epilogue, ys, extras, col_blk0 + j).astype(o_ref.dtype)


def _wproj(a_list, w_list, col0, ncols, epilogue, out_dtype, extras=(), tm=1024, tn=1024):
    n_w = len(a_list)
    m, k = a_list[0].shape
    ni = m // tm
    assert col0 % tn == 0 and ncols % tn == 0 and m % tm == 0 and ni % 2 == 0
    assert ncols // tn >= 2 and k % ni == 0
    kc = k // ni
    in_specs = ([pl.BlockSpec((tm, k), lambda j, i: (i, 0))] * n_w
                + [pl.BlockSpec(memory_space=pl.ANY)] * n_w
                + [pl.BlockSpec(bs, im) for _, bs, im in extras])
    return pl.pallas_call(
        functools.partial(_wproj_kernel, n_w=n_w, n_extra=len(extras), epilogue=epilogue,
                          kc=kc, col_blk0=col0 // tn),
        grid=(ncols // tn, ni),
        in_specs=in_specs,
        out_specs=pl.BlockSpec((tm, tn), lambda j, i: (i, j)),
        out_shape=jax.ShapeDtypeStruct((m, ncols), out_dtype),
        scratch_shapes=([pltpu.VMEM((2, k, tn), BF16)] * n_w
                        + [pltpu.VMEM((2, kc, tn), F32)] * n_w
                        + [pltpu.SemaphoreType.DMA((n_w, 2))]),
        compiler_params=_cparams(("arbitrary", "arbitrary")),
        name="proj_" + epilogue,
    )(*a_list, *w_list, *[e[0] for e in extras])


def _t5_bucket(n):
    max_exact = NUM_BUCKETS // 2
    nf = jnp.maximum(n, 1).astype(F32)
    large = max_exact + (jnp.log(nf / max_exact) / math.log(MAX_DISTANCE / max_exact)
                         * (NUM_BUCKETS - max_exact)).astype(jnp.int32)
    large = jnp.minimum(large, NUM_BUCKETS - 1)
    return jnp.where(n < max_exact, n, large)


def _attn_kernel(rb_ref, q_ref, k_ref, v_ref, zs_ref, lam_ref, g_ref, o_ref,
                 dn_ref, acc_ref, m_ref, l_ref,
                 *, lam_init):
    t = ATT_T
    dh = DA_HEAD_DIM
    h = pl.program_id(0)
    i = pl.program_id(1)

    @pl.when(i == 0)
    def _build_bias():
        r = lax.broadcasted_iota(jnp.int32, (LANE, LANE), 0)
        c = lax.broadcasted_iota(jnp.int32, (LANE, LANE), 1)
        n0 = r - c
        b0 = _t5_bucket(jnp.maximum(n0, 0))
        b1 = _t5_bucket(n0 + LANE)
        last = rb_ref[h, NUM_BUCKETS - 1]
        p0 = jnp.zeros((LANE, LANE), F32)
        p1 = jnp.zeros((LANE, LANE), F32)
        for b in range(NUM_BUCKETS):
            val = (rb_ref[h, b] - last) * math.log2(math.e)
            p0 = jnp.where(b0 == b, val, p0)
            p1 = jnp.where(b1 == b, val, p1)
        p0 = jnp.where(n0 >= 0, p0, NEG)
        nblk = t // LANE
        zero = jnp.zeros((LANE, LANE), F32)
        neg = jnp.full((LANE, LANE), NEG, F32)
        for a in range(nblk):
            for b in range(nblk):
                dn_ref[a * LANE:(a + 1) * LANE, b * LANE:(b + 1) * LANE] = (
                    p1 if (a == 0 and b == nblk - 1) else zero)
                blk = p0 if a == b else p1 if a == b + 1 else zero if a > b else neg
                dn_ref[a * LANE:(a + 1) * LANE, t + b * LANE:t + (b + 1) * LANE] = blk

    acc_ref[...] = jnp.zeros_like(acc_ref)
    m_ref[...] = jnp.full_like(m_ref, NEG)
    l_ref[...] = jnp.zeros_like(l_ref)

    def step(kb, width, bias=None):
        tk = width * t
        off = pl.multiple_of(kb * t, t)
        k = k_ref[pl.ds(off, tk), :]
        v = v_ref[pl.ds(off, tk), :]
        ss = []
        for c in range(2):
            q = q_ref[:, c * dh:(c + 1) * dh]
            s = lax.dot_general(q, k[:, c * dh:(c + 1) * dh], (((1,), (1,)), ((), ())),
                                preferred_element_type=F32)
            ss.append(s if bias is None else s + bias)
        m_out, l_out, alphas, probs = [], [], [], []
        for c in range(2):
            rows = slice(c * t, (c + 1) * t)
            m_old = m_ref[rows, :]
            m_new = jnp.maximum(m_old, jnp.max(ss[c], axis=1, keepdims=True))
            alpha = jnp.exp2(m_old - m_new)
            ps = [jnp.exp2(ss[c][:, j * LANE:(j + 1) * LANE] - m_new)
                  for j in range(tk // LANE)]
            l_out.append(alpha * l_ref[rows, :] + functools.reduce(lambda a, b: a + b, ps))
            probs.append(jnp.concatenate(ps, axis=1).astype(BF16))
            alphas.append(alpha)
            m_out.append(m_new)
        acc_out = []
        for c in range(2):
            rows = slice(c * t, (c + 1) * t)
            pv = jnp.dot(probs[c], v, preferred_element_type=F32)
            alpha_w = jnp.concatenate([alphas[c]] * (acc_ref.shape[1] // LANE), axis=1)
            acc_out.append(alpha_w * acc_ref[rows, :] + pv)
        m_ref[...] = jnp.concatenate(m_out, axis=0)
        l_ref[...] = jnp.concatenate(l_out, axis=0)
        acc_ref[...] = jnp.concatenate(acc_out, axis=0)

    n_far = jnp.maximum(i - 1, 0)

    def far_quad(j, carry):
        step(4 * j, 4)
        return carry

    lax.fori_loop(0, n_far // 4, far_quad, 0)

    @pl.when(n_far % 4 >= 2)
    def _far_pair():
        step((n_far // 4) * 4, 2)

    @pl.when(n_far % 2 == 1)
    def _far_single():
        step(n_far - 1, 1)

    @pl.when(i >= 1)
    def _below_and_on_diagonal():
        step(i - 1, 2, dn_ref[...])

    @pl.when(i == 0)
    def _diagonal_only():
        step(0, 1, dn_ref[:, t:2 * t])

    lam_v = lam_ref[...]
    lam = (jnp.exp(jnp.sum(lam_v[0:1] * lam_v[1:2], axis=1, keepdims=True))
           - jnp.exp(jnp.sum(lam_v[2:3] * lam_v[3:4], axis=1, keepdims=True)) + lam_init)
    l_sum = jnp.sum(l_ref[...], axis=1, keepdims=True)
    o = acc_ref[0:t, :] / l_sum[0:t] - lam * (acc_ref[t:2 * t, :] / l_sum[t:2 * t])
    ms = jnp.mean(o * o, axis=-1, keepdims=True)
    y = ((o * lax.rsqrt(ms + EPS)) * g_ref[...]) * (1.0 - lam_init)
    o_ref[...] = (y * zs_ref[...]).astype(o_ref.dtype)


def _diff_attention(qkv, zs, rel_bias_t, lam_vecs, subln_g, lam_init):
    s_len = qkv.shape[0]
    t = ATT_T
    hw = DA_V_DIM
    nh = DA_HEADS
    grid = (nh, s_len // t)
    return pl.pallas_call(
        functools.partial(_attn_kernel, lam_init=lam_init),
        grid=grid,
        in_specs=[
            pl.BlockSpec(memory_space=pltpu.SMEM),
            pl.BlockSpec((t, hw), lambda h, i: (i, h)),
            pl.BlockSpec((s_len, hw), lambda h, i: (0, nh + h)),
            pl.BlockSpec((s_len, hw), lambda h, i: (0, 2 * nh + h)),
            pl.BlockSpec((t, hw), lambda h, i: (i, h)),
            pl.BlockSpec((4, DA_HEAD_DIM), lambda h, i: (0, 0)),
            pl.BlockSpec((1, hw), lambda h, i: (0, 0)),
        ],
        out_specs=pl.BlockSpec((t, hw), lambda h, i: (i, h)),
        out_shape=jax.ShapeDtypeStruct((s_len, nh * hw), BF16),
        scratch_shapes=[
            pltpu.VMEM((t, 2 * t), F32),
            pltpu.VMEM((2 * t, hw), F32),
            pltpu.VMEM((2 * t, LANE), F32), pltpu.VMEM((2 * t, LANE), F32),
        ],
        compiler_params=_cparams(("arbitrary", "arbitrary")),
        name="diff_attention",
    )(rel_bias_t, qkv, qkv, qkv, zs, lam_vecs, subln_g.reshape(1, hw))


def _sgu_kernel(gv_ref, gu_ref, sz_ref, lng_ref, lnb_ref, w_ref, bt_ref, o_ref):
    v = gv_ref[...]
    mu = jnp.mean(v, axis=-1, keepdims=True)
    xc = v - mu
    var = jnp.mean(xc * xc, axis=-1, keepdims=True)
    vn = ((xc * lax.rsqrt(var + EPS)) * lng_ref[...] + lnb_ref[...]).astype(BF16)
    r = lax.broadcasted_iota(jnp.int32, (CHUNK, CHUNK), 0)
    c = lax.broadcasted_iota(jnp.int32, (CHUNK, CHUNK), 1)
    tril = r >= c
    gd = v.shape[1] // SG_GROUPS
    for g in range(SG_GROUPS):
        sl = slice(g * gd, (g + 1) * gd)
        wg = jnp.where(tril, w_ref[g], jnp.zeros((CHUNK, CHUNK), BF16))
        mixed = jnp.dot(wg, vn[:, sl], preferred_element_type=F32) + bt_ref[:, g:g + 1]
        o_ref[:, sl] = ((gu_ref[:, sl] * mixed) * sz_ref[:, sl]).astype(o_ref.dtype)


def _spatial_gate(gv, gu, sz, ln_g, ln_b, w_bf16, b_t):
    s_len, width = gv.shape
    row = pl.BlockSpec((CHUNK, width), lambda i: (i, 0))
    vec = pl.BlockSpec((1, width), lambda i: (0, 0))
    return pl.pallas_call(
        _sgu_kernel,
        grid=(s_len // CHUNK,),
        in_specs=[row, row, row, vec, vec,
                  pl.BlockSpec((SG_GROUPS, CHUNK, CHUNK), lambda i: (0, 0, 0)),
                  pl.BlockSpec((CHUNK, SG_GROUPS), lambda i: (0, 0))],
        out_specs=row,
        out_shape=jax.ShapeDtypeStruct((s_len, width), BF16),
        compiler_params=_cparams(("parallel",)),
        name="spatial_gate",
    )(gv, gu, sz, ln_g.reshape(1, width), ln_b.reshape(1, width), w_bf16, b_t)


def _mem_kv_kernel(mem_ref, g_ref, w_ref, o_ref):
    x = mem_ref[...]
    ms = jnp.mean(x * x, axis=-1, keepdims=True)
    mn = ((x * lax.rsqrt(ms + EPS)) * g_ref[...]).astype(BF16)
    o_ref[...] = jnp.dot(mn, w_ref[...], preferred_element_type=F32).astype(o_ref.dtype)


def _mem_kv(mem, g, w_xkv):
    ml, d = mem.shape
    n = w_xkv.shape[1]
    return pl.pallas_call(
        _mem_kv_kernel,
        grid=(1,),
        in_specs=[pl.BlockSpec((ml, d), lambda i: (0, 0)),
                  pl.BlockSpec((1, d), lambda i: (0, 0)),
                  pl.BlockSpec((d, n), lambda i: (0, 0))],
        out_specs=pl.BlockSpec((ml, n), lambda i: (0, 0)),
        out_shape=jax.ShapeDtypeStruct((ml, n), BF16),
        compiler_params=_cparams(("arbitrary",)),
        name="mem_kv",
    )(mem, g.reshape(1, d), w_xkv)


def _xattn_kernel(h_ref, gx_ref, wq_ref, kv_ref, wo_ref, gf_ref, o_ref):
    hd = XA_HEAD_DIM
    xw = XA_HEADS * hd
    h = h_ref[...]
    ms = jnp.mean(h * h, axis=-1, keepdims=True)
    hn = ((h * lax.rsqrt(ms + EPS)) * gx_ref[...]).astype(BF16)
    q = jnp.dot(hn, wq_ref[...], preferred_element_type=F32).astype(BF16)
    outs = []
    for a in range(XA_HEADS):
        k = kv_ref[:, a * hd:(a + 1) * hd]
        v = kv_ref[:, xw + a * hd:xw + (a + 1) * hd]
        s = lax.dot_general(q[:, a * hd:(a + 1) * hd], k, (((1,), (1,)), ((), ())),
                            preferred_element_type=F32) * (hd ** -0.5)
        p = jnp.exp(s - jnp.max(s, axis=-1, keepdims=True))
        p = p / jnp.sum(p, axis=-1, keepdims=True)
        outs.append(jnp.dot(p.astype(BF16), v, preferred_element_type=F32))
    o = jnp.concatenate(outs, axis=-1).astype(BF16)
    h2 = h + jnp.dot(o, wo_ref[...], preferred_element_type=F32)
    ms2 = jnp.mean(h2 * h2, axis=-1, keepdims=True)
    o_ref[...] = (h2 * lax.rsqrt(ms2 + EPS)) * gf_ref[...]


def _xattn_final(h, gx, wq, kv, wo, gf, tm=256):
    m, d = h.shape
    row = pl.BlockSpec((tm, d), lambda i: (i, 0))
    vec = pl.BlockSpec((1, d), lambda i: (0, 0))
    full = lambda arr: pl.BlockSpec(arr.shape, lambda i: (0, 0))
    return pl.pallas_call(
        _xattn_kernel,
        grid=(m // tm,),
        in_specs=[row, vec, full(wq), full(kv), full(wo), vec],
        out_specs=row,
        out_shape=jax.ShapeDtypeStruct((m, d), F32),
        compiler_params=_cparams(("parallel",)),
        name="xattn_final",
    )(h, gx.reshape(1, d), wq, kv, wo, gf.reshape(1, d))


def kernel(x, positions, mem, w_in, b_gate, norm1_g, lam_q1, lam_k1, lam_q2, lam_k2,
           subln_g, rel_bias, sg_ln_g, sg_ln_b, sg_w, sg_b, w_proj_a, w_proj_b, w_out,
           norm_x_g, norm_mem_g, w_xq, w_xkv, w_xo, final_g):
    bsz, s_len, d = x.shape
    assert bsz == 1 and positions.shape == (bsz, s_len)
    depth = w_in.shape[0]
    w = DA_HEADS * DA_V_DIM
    h = x.reshape(s_len, d)
    rel_bias_t = rel_bias.T
    for l in range(depth):
        lam_init = 0.8 - 0.6 * math.exp(-0.3 * l)
        hn = _rmsnorm(h, norm1_g[l], BF16)
        w_in_l = w_in[l]

        def in_proj(col0, ncols, epilogue, out_dtype, extras=()):
            return _wproj([hn], [w_in_l], col0, ncols, epilogue, out_dtype, extras)

        qkv = in_proj(0, 3 * w, "qkv", BF16)
        zs_a = in_proj(3 * w, w, "silu", F32)
        gu = in_proj(4 * w, w, "gelu", F32)
        gv = in_proj(5 * w, w, "gelu", F32)
        zs_b = in_proj(6 * w, w, "silu", F32)
        tn = 1024
        gates = in_proj(7 * w, 2 * d, "sigmoid_bias", F32, extras=[
            (b_gate[l].reshape(1, 2 * d), (1, tn), lambda j, i: (0, j))])
        lam_vecs = jnp.stack([lam_q1[l], lam_k1[l], lam_q2[l], lam_k2[l]]).astype(F32)
        a_in = _diff_attention(qkv, zs_a, rel_bias_t, lam_vecs, subln_g[l], lam_init)
        b_in = _spatial_gate(gv, gu, zs_b, sg_ln_g[l], sg_ln_b[l],
                             sg_w[l].astype(BF16), sg_b[l].T)
        tm, tn = 512, 512
        nb = d // tn
        merged = _wproj([a_in, b_in], [w_proj_a[l], w_proj_b[l]], 0, d, "gated_sum", BF16,
                        extras=[(gates, (tm, tn), lambda j, i: (i, j)),
                                (gates, (tm, tn), lambda j, i, nb=nb: (i, j + nb))],
                        tm=tm, tn=tn)
        tm, tn = 1024, 512
        h = _wproj([merged], [w_out[l]], 0, d, "residual", F32,
                   extras=[(h, (tm, tn), lambda j, i: (i, j))], tm=tm, tn=tn)
        kv = _mem_kv(mem.reshape(-1, d), norm_mem_g[l], w_xkv[l].astype(BF16))
        last = l == depth - 1
        assert last, "stacked layers would need an un-normalised hand-off"
        h = _xattn_final(h, norm_x_g[l], w_xq[l].astype(BF16), kv, w_xo[l].astype(BF16), final_g)
    return h.reshape(bsz, s_len, d)
```

```python
import functools
import math

import jax
import jax.numpy as jnp
from jax import lax
from jax.experimental import pallas as pl
from jax.experimental.pallas import tpu as pltpu

F32 = jnp.float32
BF16 = jnp.bfloat16

EPS = 1e-6
LANE = 128
VMEM_LIMIT = 56 * 1024 * 1024

DA_HEADS = 16
DA_HEAD_DIM = 128
DA_V_DIM = 2 * DA_HEAD_DIM
SG_GROUPS = 16
CHUNK = 128
NUM_BUCKETS = 32
MAX_DISTANCE = 128
XA_HEADS = 4
XA_HEAD_DIM = 128

ATT_T = 512
SOFTMAX_C2 = DA_HEAD_DIM ** -0.5 * math.log2(math.e)
NEG = -1e30


def _cparams(sem):
    return pltpu.CompilerParams(dimension_semantics=sem, vmem_limit_bytes=VMEM_LIMIT)


def _sigmoid(x):
    return 1.0 / (1.0 + jnp.exp(-x))


def _gelu_tanh(x):
    c = math.sqrt(2.0 / math.pi)
    return 0.5 * x * (1.0 + jnp.tanh(c * (x + 0.044715 * (x * x * x))))


def _rmsnorm_kernel(x_ref, g_ref, o_ref):
    x = x_ref[...]
    ms = jnp.mean(x * x, axis=-1, keepdims=True)
    o_ref[...] = ((x * lax.rsqrt(ms + EPS)) * g_ref[...]).astype(o_ref.dtype)


def _rmsnorm(x, g, out_dtype, rows=512):
    m, d = x.shape
    return pl.pallas_call(
        _rmsnorm_kernel,
        grid=(m // rows,),
        in_specs=[pl.BlockSpec((rows, d), lambda i: (i, 0)),
                  pl.BlockSpec((1, d), lambda i: (0, 0))],
        out_specs=pl.BlockSpec((rows, d), lambda i: (i, 0)),
        out_shape=jax.ShapeDtypeStruct((m, d), out_dtype),
        compiler_params=_cparams(("parallel",)),
        name="rmsnorm",
    )(x, g.reshape(1, d))


def _epilogue(kind, ys, extras, col_tile):
    if kind == "cast":
        return ys[0]
    if kind == "qkv":
        q_tiles = DA_HEADS * DA_V_DIM // ys[0].shape[1]
        return ys[0] * jnp.where(col_tile < q_tiles, SOFTMAX_C2, 1.0)
    if kind == "silu":
        return ys[0] * _sigmoid(ys[0])
    if kind == "gelu":
        return _gelu_tanh(ys[0])
    if kind == "sigmoid_bias":
        return _sigmoid(ys[0] + extras[0][...])
    if kind == "residual":
        return extras[0][...] + ys[0]
    if kind == "gated_sum":
        return extras[0][...] * ys[0] + extras[1][...] * ys[1]
    if kind == "gelu_silu_product":
        return _gelu_tanh(ys[0]) * (ys[1] * _sigmoid(ys[1]))
    raise ValueError(kind)


def _wproj_kernel(*refs, n_a, n_w, n_extra, epilogue, kc, col_blk0):
    a_refs = refs[:n_a]
    refs = refs[n_a:]
    w_refs = refs[:n_w]
    extras = refs[n_w:n_w + n_extra]
    o_ref = refs[n_w + n_extra]
    wb_refs = refs[n_w + n_extra + 1:2 * n_w + n_extra + 1]
    st_refs = refs[2 * n_w + n_extra + 1:3 * n_w + n_extra + 1]
    sem = refs[-1]
    j = pl.program_id(0)
    i = pl.program_id(1)
    nj = pl.num_programs(0)
    ni = pl.num_programs(1)
    tn = o_ref.shape[1]

    def chunk_copy(w, jt, c, slot):
        return pltpu.make_async_copy(
            w_refs[w].at[pl.ds(c * kc, kc), pl.ds((col_blk0[w] + jt) * tn, tn)],
            st_refs[w].at[slot], sem.at[w, slot])

    def cast_chunk(w, c, slot, wslot):
        row0 = c * kc if isinstance(c, int) else pl.multiple_of(c * kc, kc)
        wb_refs[w][wslot, pl.ds(row0, kc), :] = st_refs[w][slot].astype(BF16)

    n_chunks = w_refs[0].shape[0] // kc

    @pl.when((j == 0) & (i == 0))
    def _first_tile():
        for w in range(n_w):
            chunk_copy(w, 0, 0, 0).start()
        for c in range(n_chunks):
            for w in range(n_w):
                if c + 1 < n_chunks:
                    chunk_copy(w, 0, c + 1, (c + 1) % 2).start()
                chunk_copy(w, 0, c, c % 2).wait()
                cast_chunk(w, c, c % 2, 0)
        for w in range(n_w):
            chunk_copy(w, 1 % nj, 0, 0).start()

    slot = i % 2
    cur = j % 2
    nxt = 1 - cur
    next_tile = (j + 1) % nj

    @pl.when(jnp.logical_not((j == nj - 1) & (i == ni - 1)))
    def _prefetch():
        wrap = i == ni - 1
        jt = jnp.where(wrap, (j + 2) % nj, next_tile)
        c = jnp.where(wrap, 0, i + 1)
        for w in range(n_w):
            chunk_copy(w, jt, c, 1 - slot).start()

    for w in range(n_w):
        chunk_copy(w, next_tile, i, slot).wait()
        cast_chunk(w, i, slot, nxt)

    ys = [jnp.dot(a_refs[w % n_a][...], wb_refs[w][cur], preferred_element_type=F32)
          for w in range(n_w)]
    o_ref[...] = _epilogue(epilogue, ys, extras, col_blk0[0] + j).astype(o_ref.dtype)


def _wproj(a_list, w_list, col0s, ncols, epilogue, out_dtype, extras=(), tm=1024, tn=1024):
    n_a, n_w = len(a_list), len(w_list)
    m, k = a_list[0].shape
    ni = m // tm
    assert n_a in (1, n_w) and len(col0s) == n_w and all(c % tn == 0 for c in col0s)
    assert ncols % tn == 0 and m % tm == 0 and ni % 2 == 0
    assert ncols // tn >= 2 and k % ni == 0
    kc = k // ni
    in_specs = ([pl.BlockSpec((tm, k), lambda j, i: (i, 0))] * n_a
                + [pl.BlockSpec(memory_space=pl.ANY)] * n_w
                + [pl.BlockSpec(bs, im) for _, bs, im in extras])
    return pl.pallas_call(
        functools.partial(_wproj_kernel, n_a=n_a, n_w=n_w, n_extra=len(extras),
                          epilogue=epilogue, kc=kc, col_blk0=tuple(c // tn for c in col0s)),
        grid=(ncols // tn, ni),
        in_specs=in_specs,
        out_specs=pl.BlockSpec((tm, tn), lambda j, i: (i, j)),
        out_shape=jax.ShapeDtypeStruct((m, ncols), out_dtype),
        scratch_shapes=([pltpu.VMEM((2, k, tn), BF16)] * n_w
                        + [pltpu.VMEM((2, kc, tn), F32)] * n_w
                        + [pltpu.SemaphoreType.DMA((n_w, 2))]),
        compiler_params=_cparams(("arbitrary", "arbitrary")),
        name="proj_" + epilogue,
    )(*a_list, *w_list, *[e[0] for e in extras])


def _t5_bucket(n):
    max_exact = NUM_BUCKETS // 2
    nf = jnp.maximum(n, 1).astype(F32)
    large = max_exact + (jnp.log(nf / max_exact) / math.log(MAX_DISTANCE / max_exact)
                         * (NUM_BUCKETS - max_exact)).astype(jnp.int32)
    large = jnp.minimum(large, NUM_BUCKETS - 1)
    return jnp.where(n < max_exact, n, large)


def _attn_kernel(rb_ref, q_ref, k_ref, v_ref, zs_ref, lam_ref, g_ref, o_ref,
                 dn_ref, acc_ref, m_ref, l_ref,
                 *, lam_init):
    t = ATT_T
    dh = DA_HEAD_DIM
    h = pl.program_id(0)
    i = pl.program_id(1)

    @pl.when(i == 0)
    def _build_bias():
        r = lax.broadcasted_iota(jnp.int32, (LANE, LANE), 0)
        c = lax.broadcasted_iota(jnp.int32, (LANE, LANE), 1)
        n0 = r - c
        b0 = _t5_bucket(jnp.maximum(n0, 0))
        b1 = _t5_bucket(n0 + LANE)
        last = rb_ref[h, NUM_BUCKETS - 1]
        p0 = jnp.zeros((LANE, LANE), F32)
        p1 = jnp.zeros((LANE, LANE), F32)
        for b in range(NUM_BUCKETS):
            val = (rb_ref[h, b] - last) * math.log2(math.e)
            p0 = jnp.where(b0 == b, val, p0)
            p1 = jnp.where(b1 == b, val, p1)
        p0 = jnp.where(n0 >= 0, p0, NEG)
        nblk = t // LANE
        zero = jnp.zeros((LANE, LANE), F32)
        neg = jnp.full((LANE, LANE), NEG, F32)
        for a in range(nblk):
            for b in range(nblk):
                dn_ref[a * LANE:(a + 1) * LANE, b * LANE:(b + 1) * LANE] = (
                    p1 if (a == 0 and b == nblk - 1) else zero)
                blk = p0 if a == b else p1 if a == b + 1 else zero if a > b else neg
                dn_ref[a * LANE:(a + 1) * LANE, t + b * LANE:t + (b + 1) * LANE] = blk

    acc_ref[...] = jnp.zeros_like(acc_ref)
    m_ref[...] = jnp.full_like(m_ref, NEG)
    l_ref[...] = jnp.zeros_like(l_ref)

    def step(kb, width, bias_col0=None):
        tk = width * t
        off = pl.multiple_of(kb * t, t)
        k = k_ref[pl.ds(off, tk), :]
        v = v_ref[pl.ds(off, tk), :]
        n_bias = 0 if bias_col0 is None else dn_ref.shape[1] - bias_col0
        ss = []
        for c in range(2):
            q = q_ref[:, c * dh:(c + 1) * dh]
            s = lax.dot_general(q, k[:, c * dh:(c + 1) * dh], (((1,), (1,)), ((), ())),
                                preferred_element_type=F32)
            pieces = []
            for j in range(tk // LANE):
                piece = s[:, j * LANE:(j + 1) * LANE]
                b0 = j * LANE - (tk - n_bias)
                if b0 >= 0:
                    piece = piece + dn_ref[:, bias_col0 + b0:bias_col0 + b0 + LANE]
                pieces.append(piece)
            ss.append(pieces)
        m_out, l_out, alphas, probs = [], [], [], []
        for c in range(2):
            rows = slice(c * t, (c + 1) * t)
            m_old = m_ref[rows, :]
            m_cur = functools.reduce(jnp.maximum, ss[c])
            m_new = jnp.maximum(m_old, jnp.max(m_cur, axis=1, keepdims=True))
            alpha = jnp.exp2(m_old - m_new)
            ps = [jnp.exp2(piece - m_new) for piece in ss[c]]
            l_out.append(alpha * l_ref[rows, :] + functools.reduce(lambda a, b: a + b, ps))
            probs.append(jnp.concatenate(ps, axis=1).astype(BF16))
            alphas.append(alpha)
            m_out.append(m_new)
        acc_out = []
        for c in range(2):
            rows = slice(c * t, (c + 1) * t)
            pv = jnp.dot(probs[c], v, preferred_element_type=F32)
            alpha_w = jnp.concatenate([alphas[c]] * (acc_ref.shape[1] // LANE), axis=1)
            acc_out.append(alpha_w * acc_ref[rows, :] + pv)
        m_ref[...] = jnp.concatenate(m_out, axis=0)
        l_ref[...] = jnp.concatenate(l_out, axis=0)
        acc_ref[...] = jnp.concatenate(acc_out, axis=0)

    n_far = jnp.maximum(i - 1, 0)

    def far_quad(j, carry):
        step(4 * j, 4)
        return carry

    lax.fori_loop(0, n_far // 4, far_quad, 0)

    for rem in range(4):
        @pl.when((i >= 1) & (n_far % 4 == rem))
        def _tail(rem=rem):
            step(n_far - rem, rem + 2, bias_col0=0)

    @pl.when(i == 0)
    def _diagonal_only():
        step(0, 1, bias_col0=t)

    lam_v = lam_ref[...]
    lam = (jnp.exp(jnp.sum(lam_v[0:1] * lam_v[1:2], axis=1, keepdims=True))
           - jnp.exp(jnp.sum(lam_v[2:3] * lam_v[3:4], axis=1, keepdims=True)) + lam_init)
    l_sum = jnp.sum(l_ref[...], axis=1, keepdims=True)
    o = acc_ref[0:t, :] / l_sum[0:t] - lam * (acc_ref[t:2 * t, :] / l_sum[t:2 * t])
    ms = jnp.mean(o * o, axis=-1, keepdims=True)
    y = ((o * lax.rsqrt(ms + EPS)) * g_ref[...]) * (1.0 - lam_init)
    o_ref[...] = (y * zs_ref[...]).astype(o_ref.dtype)


def _diff_attention(qkv, zs, rel_bias_t, lam_vecs, subln_g, lam_init):
    s_len = qkv.shape[0]
    t = ATT_T
    hw = DA_V_DIM
    nh = DA_HEADS
    grid = (nh, s_len // t)
    return pl.pallas_call(
        functools.partial(_attn_kernel, lam_init=lam_init),
        grid=grid,
        in_specs=[
            pl.BlockSpec(memory_space=pltpu.SMEM),
            pl.BlockSpec((t, hw), lambda h, i: (i, h)),
            pl.BlockSpec((s_len, hw), lambda h, i: (0, nh + h)),
            pl.BlockSpec((s_len, hw), lambda h, i: (0, 2 * nh + h)),
            pl.BlockSpec((t, hw), lambda h, i: (i, h)),
            pl.BlockSpec((4, DA_HEAD_DIM), lambda h, i: (0, 0)),
            pl.BlockSpec((1, hw), lambda h, i: (0, 0)),
        ],
        out_specs=pl.BlockSpec((t, hw), lambda h, i: (i, h)),
        out_shape=jax.ShapeDtypeStruct((s_len, nh * hw), BF16),
        scratch_shapes=[
            pltpu.VMEM((t, 2 * t), F32),
            pltpu.VMEM((2 * t, hw), F32),
            pltpu.VMEM((2 * t, LANE), F32), pltpu.VMEM((2 * t, LANE), F32),
        ],
        compiler_params=_cparams(("arbitrary", "arbitrary")),
        name="diff_attention",
    )(rel_bias_t, qkv, qkv, qkv, zs, lam_vecs, subln_g.reshape(1, hw))


def _sgu_kernel(gv_ref, guz_ref, lng_ref, lnb_ref, w_ref, bt_ref, o_ref):
    r = lax.broadcasted_iota(jnp.int32, (CHUNK, CHUNK), 0)
    c = lax.broadcasted_iota(jnp.int32, (CHUNK, CHUNK), 1)
    tril = r >= c
    gd = gv_ref.shape[1] // SG_GROUPS
    for ch in range(gv_ref.shape[0] // CHUNK):
        rows = slice(ch * CHUNK, (ch + 1) * CHUNK)
        v = gv_ref[rows, :]
        mu = jnp.mean(v, axis=-1, keepdims=True)
        xc = v - mu
        var = jnp.mean(xc * xc, axis=-1, keepdims=True)
        vn = ((xc * lax.rsqrt(var + EPS)) * lng_ref[...] + lnb_ref[...]).astype(BF16)
        for g in range(SG_GROUPS):
            sl = slice(g * gd, (g + 1) * gd)
            wg = jnp.where(tril, w_ref[g], jnp.zeros((CHUNK, CHUNK), BF16))
            mixed = jnp.dot(wg, vn[:, sl], preferred_element_type=F32) + bt_ref[:, g:g + 1]
            o_ref[rows, sl] = (guz_ref[rows, sl] * mixed).astype(o_ref.dtype)


def _spatial_gate(gv, guz, ln_g, ln_b, w_bf16, b_t, chunks_per_step=2):
    s_len, width = gv.shape
    rows = chunks_per_step * CHUNK
    row = pl.BlockSpec((rows, width), lambda i: (i, 0))
    vec = pl.BlockSpec((1, width), lambda i: (0, 0))
    return pl.pallas_call(
        _sgu_kernel,
        grid=(s_len // rows,),
        in_specs=[row, row, vec, vec,
                  pl.BlockSpec((SG_GROUPS, CHUNK, CHUNK), lambda i: (0, 0, 0)),
                  pl.BlockSpec((CHUNK, SG_GROUPS), lambda i: (0, 0))],
        out_specs=row,
        out_shape=jax.ShapeDtypeStruct((s_len, width), BF16),
        compiler_params=_cparams(("parallel",)),
        name="spatial_gate",
    )(gv, guz, ln_g.reshape(1, width), ln_b.reshape(1, width), w_bf16, b_t)


def _mem_kv_kernel(mem_ref, g_ref, w_ref, o_ref):
    x = mem_ref[...]
    ms = jnp.mean(x * x, axis=-1, keepdims=True)
    mn = ((x * lax.rsqrt(ms + EPS)) * g_ref[...]).astype(BF16)
    o_ref[...] = jnp.dot(mn, w_ref[...], preferred_element_type=F32).astype(o_ref.dtype)


def _mem_kv(mem, g, w_xkv):
    ml, d = mem.shape
    n = w_xkv.shape[1]
    return pl.pallas_call(
        _mem_kv_kernel,
        grid=(1,),
        in_specs=[pl.BlockSpec((ml, d), lambda i: (0, 0)),
                  pl.BlockSpec((1, d), lambda i: (0, 0)),
                  pl.BlockSpec((d, n), lambda i: (0, 0))],
        out_specs=pl.BlockSpec((ml, n), lambda i: (0, 0)),
        out_shape=jax.ShapeDtypeStruct((ml, n), BF16),
        compiler_params=_cparams(("arbitrary",)),
        name="mem_kv",
    )(mem, g.reshape(1, d), w_xkv)


def _xattn_kernel(h_ref, gx_ref, wq_ref, kv_ref, wo_ref, gf_ref, o_ref):
    hd = XA_HEAD_DIM
    xw = XA_HEADS * hd
    half = h_ref.shape[0] // 2
    parts = [slice(0, half), slice(half, 2 * half)]
    hs = [h_ref[r, :] for r in parts]
    hns = []
    for h in hs:
        ms = jnp.mean(h * h, axis=-1, keepdims=True)
        hns.append(((h * lax.rsqrt(ms + EPS)) * gx_ref[...]).astype(BF16))
    qs = [jnp.dot(hn, wq_ref[...], preferred_element_type=F32).astype(BF16) for hn in hns]
    os_ = []
    for q in qs:
        outs = []
        for a in range(XA_HEADS):
            k = kv_ref[:, a * hd:(a + 1) * hd]
            v = kv_ref[:, xw + a * hd:xw + (a + 1) * hd]
            s = lax.dot_general(q[:, a * hd:(a + 1) * hd], k, (((1,), (1,)), ((), ())),
                                preferred_element_type=F32) * (hd ** -0.5)
            p = jnp.exp(s - jnp.max(s, axis=-1, keepdims=True))
            p = p / jnp.sum(p, axis=-1, keepdims=True)
            outs.append(jnp.dot(p.astype(BF16), v, preferred_element_type=F32))
        os_.append(jnp.concatenate(outs, axis=-1).astype(BF16))
    h2s = [h + jnp.dot(o, wo_ref[...], preferred_element_type=F32) for h, o in zip(hs, os_)]
    outs = []
    for h2 in h2s:
        ms2 = jnp.mean(h2 * h2, axis=-1, keepdims=True)
        outs.append((h2 * lax.rsqrt(ms2 + EPS)) * gf_ref[...])
    o_ref[...] = jnp.concatenate(outs, axis=0)


def _xattn_final(h, gx, wq, kv, wo, gf, tm=512):
    m, d = h.shape
    row = pl.BlockSpec((tm, d), lambda i: (i, 0))
    vec = pl.BlockSpec((1, d), lambda i: (0, 0))
    full = lambda arr: pl.BlockSpec(arr.shape, lambda i: (0, 0),
                                    pipeline_mode=pl.Buffered(buffer_count=1))
    return pl.pallas_call(
        _xattn_kernel,
        grid=(m // tm,),
        in_specs=[row, vec, full(wq), full(kv), full(wo), vec],
        out_specs=row,
        out_shape=jax.ShapeDtypeStruct((m, d), F32),
        compiler_params=_cparams(("parallel",)),
        name="xattn_final",
    )(h, gx.reshape(1, d), wq, kv, wo, gf.reshape(1, d))


def kernel(x, positions, mem, w_in, b_gate, norm1_g, lam_q1, lam_k1, lam_q2, lam_k2,
           subln_g, rel_bias, sg_ln_g, sg_ln_b, sg_w, sg_b, w_proj_a, w_proj_b, w_out,
           norm_x_g, norm_mem_g, w_xq, w_xkv, w_xo, final_g):
    bsz, s_len, d = x.shape
    assert bsz == 1 and positions.shape == (bsz, s_len)
    depth = w_in.shape[0]
    assert depth == 1
    w = DA_HEADS * DA_V_DIM
    h = x.reshape(s_len, d)
    rel_bias_t = rel_bias.T
    for l in range(depth):
        lam_init = 0.8 - 0.6 * math.exp(-0.3 * l)
        hn = _rmsnorm(h, norm1_g[l], BF16)
        w_in_l = w_in[l]

        def in_proj(col0, ncols, epilogue, out_dtype, extras=()):
            return _wproj([hn], [w_in_l], [col0], ncols, epilogue, out_dtype, extras)

        qkv = in_proj(0, 3 * w, "qkv", BF16)
        zs_a = in_proj(3 * w, w, "silu", F32)
        gv = in_proj(5 * w, w, "gelu", F32)
        guz = _wproj([hn], [w_in_l, w_in_l], [4 * w, 6 * w], w, "gelu_silu_product", F32,
                     tn=512)
        tn = 1024
        gates = in_proj(7 * w, 2 * d, "sigmoid_bias", F32, extras=[
            (b_gate[l].reshape(1, 2 * d), (1, tn), lambda j, i: (0, j))])
        lam_vecs = jnp.stack([lam_q1[l], lam_k1[l], lam_q2[l], lam_k2[l]]).astype(F32)
        a_in = _diff_attention(qkv, zs_a, rel_bias_t, lam_vecs, subln_g[l], lam_init)
        b_in = _spatial_gate(gv, guz, sg_ln_g[l], sg_ln_b[l],
                             sg_w[l].astype(BF16), sg_b[l].T)
        tm, tn = 512, 512
        nb = d // tn
        merged = _wproj([a_in, b_in], [w_proj_a[l], w_proj_b[l]], [0, 0], d, "gated_sum", BF16,
                        extras=[(gates, (tm, tn), lambda j, i: (i, j)),
                                (gates, (tm, tn), lambda j, i, nb=nb: (i, j + nb))],
                        tm=tm, tn=tn)
        tm, tn = 1024, 512
        h = _wproj([merged], [w_out[l]], [0], d, "residual", F32,
                   extras=[(h, (tm, tn), lambda j, i: (i, j))], tm=tm, tn=tn)
        kv = _mem_kv(mem.reshape(-1, d), norm_mem_g[l], w_xkv[l].astype(BF16))
        h = _xattn_final(h, norm_x_g[l], w_xq[l].astype(BF16), kv, w_xo[l].astype(BF16), final_g)
    return h.reshape(bsz, s_len, d)
```

```python
import functools
import math

import jax
import jax.numpy as jnp
from jax import lax
from jax.experimental import pallas as pl
from jax.experimental.pallas import tpu as pltpu

F32 = jnp.float32
BF16 = jnp.bfloat16

EPS = 1e-6
LANE = 128
VMEM_LIMIT = 56 * 1024 * 1024

DA_HEADS = 16
DA_HEAD_DIM = 128
DA_V_DIM = 2 * DA_HEAD_DIM
SG_GROUPS = 16
CHUNK = 128
NUM_BUCKETS = 32
MAX_DISTANCE = 128
XA_HEADS = 4
XA_HEAD_DIM = 128

ATT_T = 512
SOFTMAX_C2 = DA_HEAD_DIM ** -0.5 * math.log2(math.e)
NEG = -1e30


def _cparams(sem):
    return pltpu.CompilerParams(dimension_semantics=sem, vmem_limit_bytes=VMEM_LIMIT)


def _sigmoid(x):
    return 0.5 * jnp.tanh(0.5 * x) + 0.5


def _silu(x):
    h = 0.5 * x
    return h * jnp.tanh(h) + h


def _gelu_tanh(x):
    c = math.sqrt(2.0 / math.pi)
    h = 0.5 * x
    return h * jnp.tanh(x * (c + (c * 0.044715) * (x * x))) + h


def _rmsnorm_kernel(x_ref, g_ref, o_ref):
    x = x_ref[...]
    ms = jnp.mean(x * x, axis=-1, keepdims=True)
    o_ref[...] = ((x * lax.rsqrt(ms + EPS)) * g_ref[...]).astype(o_ref.dtype)


def _rmsnorm(x, g, out_dtype, rows=512):
    m, d = x.shape
    return pl.pallas_call(
        _rmsnorm_kernel,
        grid=(m // rows,),
        in_specs=[pl.BlockSpec((rows, d), lambda i: (i, 0)),
                  pl.BlockSpec((1, d), lambda i: (0, 0))],
        out_specs=pl.BlockSpec((rows, d), lambda i: (i, 0)),
        out_shape=jax.ShapeDtypeStruct((m, d), out_dtype),
        compiler_params=_cparams(("parallel",)),
        name="rmsnorm",
    )(x, g.reshape(1, d))


def _epilogue(kind, ys, extras, col_tile):
    if kind == "cast":
        return ys[0]
    if kind == "qkv":
        q_tiles = DA_HEADS * DA_V_DIM // ys[0].shape[1]
        return ys[0] * jnp.where(col_tile < q_tiles, SOFTMAX_C2, 1.0)
    if kind == "silu":
        return _silu(ys[0])
    if kind == "gelu":
        return _gelu_tanh(ys[0])
    if kind == "sigmoid_bias":
        return _sigmoid(ys[0] + extras[0][...])
    if kind == "residual":
        return extras[0][...] + ys[0]
    if kind == "gated_sum":
        return extras[0][...] * ys[0] + extras[1][...] * ys[1]
    if kind == "gelu_silu_product":
        return _gelu_tanh(ys[0]) * _silu(ys[1])
    raise ValueError(kind)


def _wproj_kernel(*refs, n_a, n_w, n_extra, epilogue, kc, col_blk0):
    a_refs = refs[:n_a]
    refs = refs[n_a:]
    w_refs = refs[:n_w]
    extras = refs[n_w:n_w + n_extra]
    o_ref = refs[n_w + n_extra]
    wb_refs = refs[n_w + n_extra + 1:2 * n_w + n_extra + 1]
    st_refs = refs[2 * n_w + n_extra + 1:3 * n_w + n_extra + 1]
    sem = refs[-1]
    j = pl.program_id(0)
    i = pl.program_id(1)
    nj = pl.num_programs(0)
    ni = pl.num_programs(1)
    tn = o_ref.shape[1]

    def chunk_copy(w, jt, c, slot):
        return pltpu.make_async_copy(
            w_refs[w].at[pl.ds(c * kc, kc), pl.ds((col_blk0[w] + jt) * tn, tn)],
            st_refs[w].at[slot], sem.at[w, slot])

    def cast_chunk(w, c, slot, wslot):
        row0 = c * kc if isinstance(c, int) else pl.multiple_of(c * kc, kc)
        wb_refs[w][wslot, pl.ds(row0, kc), :] = st_refs[w][slot].astype(BF16)

    n_chunks = w_refs[0].shape[0] // kc

    @pl.when((j == 0) & (i == 0))
    def _first_tile():
        for w in range(n_w):
            chunk_copy(w, 0, 0, 0).start()
        for c in range(n_chunks):
            for w in range(n_w):
                if c + 1 < n_chunks:
                    chunk_copy(w, 0, c + 1, (c + 1) % 2).start()
                chunk_copy(w, 0, c, c % 2).wait()
                cast_chunk(w, c, c % 2, 0)
        for w in range(n_w):
            chunk_copy(w, 1 % nj, 0, 0).start()

    slot = i % 2
    cur = j % 2
    nxt = 1 - cur
    next_tile = (j + 1) % nj

    @pl.when(jnp.logical_not((j == nj - 1) & (i == ni - 1)))
    def _prefetch():
        wrap = i == ni - 1
        jt = jnp.where(wrap, (j + 2) % nj, next_tile)
        c = jnp.where(wrap, 0, i + 1)
        for w in range(n_w):
            chunk_copy(w, jt, c, 1 - slot).start()

    for w in range(n_w):
        chunk_copy(w, next_tile, i, slot).wait()
        cast_chunk(w, i, slot, nxt)

    ys = [jnp.dot(a_refs[w % n_a][...], wb_refs[w][cur], preferred_element_type=F32)
          for w in range(n_w)]
    o_ref[...] = _epilogue(epilogue, ys, extras, col_blk0[0] + j).astype(o_ref.dtype)


def _wproj(a_list, w_list, col0s, ncols, epilogue, out_dtype, extras=(), tm=1024, tn=1024):
    n_a, n_w = len(a_list), len(w_list)
    m, k = a_list[0].shape
    ni = m // tm
    assert n_a in (1, n_w) and len(col0s) == n_w and all(c % tn == 0 for c in col0s)
    assert ncols % tn == 0 and m % tm == 0 and ni % 2 == 0
    assert ncols // tn >= 2 and k % ni == 0
    kc = k // ni
    in_specs = ([pl.BlockSpec((tm, k), lambda j, i: (i, 0))] * n_a
                + [pl.BlockSpec(memory_space=pl.ANY)] * n_w
                + [pl.BlockSpec(bs, im) for _, bs, im in extras])
    return pl.pallas_call(
        functools.partial(_wproj_kernel, n_a=n_a, n_w=n_w, n_extra=len(extras),
                          epilogue=epilogue, kc=kc, col_blk0=tuple(c // tn for c in col0s)),
        grid=(ncols // tn, ni),
        in_specs=in_specs,
        out_specs=pl.BlockSpec((tm, tn), lambda j, i: (i, j)),
        out_shape=jax.ShapeDtypeStruct((m, ncols), out_dtype),
        scratch_shapes=([pltpu.VMEM((2, k, tn), BF16)] * n_w
                        + [pltpu.VMEM((2, kc, tn), F32)] * n_w
                        + [pltpu.SemaphoreType.DMA((n_w, 2))]),
        compiler_params=_cparams(("arbitrary", "arbitrary")),
        name="proj_" + epilogue,
    )(*a_list, *w_list, *[e[0] for e in extras])


def _t5_bucket(n):
    max_exact = NUM_BUCKETS // 2
    nf = jnp.maximum(n, 1).astype(F32)
    large = max_exact + (jnp.log(nf / max_exact) / math.log(MAX_DISTANCE / max_exact)
                         * (NUM_BUCKETS - max_exact)).astype(jnp.int32)
    large = jnp.minimum(large, NUM_BUCKETS - 1)
    return jnp.where(n < max_exact, n, large)


def _attn_kernel(rb_ref, q_ref, k_ref, v_ref, zs_ref, lam_ref, g_ref, o_ref,
                 dn_ref, acc_ref, m_ref, l_ref,
                 *, lam_init):
    t = ATT_T
    dh = DA_HEAD_DIM
    h = pl.program_id(0)
    i = pl.program_id(1)

    @pl.when(i == 0)
    def _build_bias():
        r = lax.broadcasted_iota(jnp.int32, (LANE, LANE), 0)
        c = lax.broadcasted_iota(jnp.int32, (LANE, LANE), 1)
        n0 = r - c
        b0 = _t5_bucket(jnp.maximum(n0, 0))
        b1 = _t5_bucket(n0 + LANE)
        last = rb_ref[h, NUM_BUCKETS - 1]
        p0 = jnp.zeros((LANE, LANE), F32)
        p1 = jnp.zeros((LANE, LANE), F32)
        for b in range(NUM_BUCKETS):
            val = (rb_ref[h, b] - last) * math.log2(math.e)
            p0 = jnp.where(b0 == b, val, p0)
            p1 = jnp.where(b1 == b, val, p1)
        p0 = jnp.where(n0 >= 0, p0, NEG)
        nblk = t // LANE
        zero = jnp.zeros((LANE, LANE), F32)
        neg = jnp.full((LANE, LANE), NEG, F32)
        for a in range(nblk):
            for b in range(nblk):
                dn_ref[a * LANE:(a + 1) * LANE, b * LANE:(b + 1) * LANE] = (
                    p1 if (a == 0 and b == nblk - 1) else zero)
                blk = p0 if a == b else p1 if a == b + 1 else zero if a > b else neg
                dn_ref[a * LANE:(a + 1) * LANE, t + b * LANE:t + (b + 1) * LANE] = blk

    acc_ref[...] = jnp.zeros_like(acc_ref)
    m_ref[...] = jnp.full_like(m_ref, NEG)
    l_ref[...] = jnp.zeros_like(l_ref)

    def step(kb, width, bias_col0=None):
        tk = width * t
        off = pl.multiple_of(kb * t, t)
        k = k_ref[pl.ds(off, tk), :]
        v = v_ref[pl.ds(off, tk), :]
        n_bias = 0 if bias_col0 is None else dn_ref.shape[1] - bias_col0
        ss = []
        for c in range(2):
            q = q_ref[:, c * dh:(c + 1) * dh]
            s = lax.dot_general(q, k[:, c * dh:(c + 1) * dh], (((1,), (1,)), ((), ())),
                                preferred_element_type=F32)
            pieces = []
            for j in range(tk // LANE):
                piece = s[:, j * LANE:(j + 1) * LANE]
                b0 = j * LANE - (tk - n_bias)
                if b0 >= 0:
                    piece = piece + dn_ref[:, bias_col0 + b0:bias_col0 + b0 + LANE]
                pieces.append(piece)
            ss.append(pieces)
        m_out, l_out, alphas, probs = [], [], [], []
        for c in range(2):
            rows = slice(c * t, (c + 1) * t)
            m_old = m_ref[rows, :]
            m_cur = functools.reduce(jnp.maximum, ss[c])
            m_new = jnp.maximum(m_old, jnp.max(m_cur, axis=1, keepdims=True))
            alpha = jnp.exp2(m_old - m_new)
            ps = [jnp.exp2(piece - m_new) for piece in ss[c]]
            l_out.append(alpha * l_ref[rows, :] + functools.reduce(lambda a, b: a + b, ps))
            probs.append(jnp.concatenate(ps, axis=1).astype(BF16))
            alphas.append(alpha)
            m_out.append(m_new)
        acc_out = []
        for c in range(2):
            rows = slice(c * t, (c + 1) * t)
            pv = jnp.dot(probs[c], v, preferred_element_type=F32)
            alpha_w = jnp.concatenate([alphas[c]] * (acc_ref.shape[1] // LANE), axis=1)
            acc_out.append(alpha_w * acc_ref[rows, :] + pv)
        m_ref[...] = jnp.concatenate(m_out, axis=0)
        l_ref[...] = jnp.concatenate(l_out, axis=0)
        acc_ref[...] = jnp.concatenate(acc_out, axis=0)

    n_far = jnp.maximum(i - 1, 0)

    def far_quad(j, carry):
        step(4 * j, 4)
        return carry

    lax.fori_loop(0, n_far // 4, far_quad, 0)

    for rem in range(4):
        @pl.when((i >= 1) & (n_far % 4 == rem))
        def _tail(rem=rem):
            step(n_far - rem, rem + 2, bias_col0=0)

    @pl.when(i == 0)
    def _diagonal_only():
        step(0, 1, bias_col0=t)

    lam_v = lam_ref[...]
    lam = (jnp.exp(jnp.sum(lam_v[0:1] * lam_v[1:2], axis=1, keepdims=True))
           - jnp.exp(jnp.sum(lam_v[2:3] * lam_v[3:4], axis=1, keepdims=True)) + lam_init)
    l_sum = jnp.sum(l_ref[...], axis=1, keepdims=True)
    o = acc_ref[0:t, :] / l_sum[0:t] - lam * (acc_ref[t:2 * t, :] / l_sum[t:2 * t])
    ms = jnp.mean(o * o, axis=-1, keepdims=True)
    y = ((o * lax.rsqrt(ms + EPS)) * g_ref[...]) * (1.0 - lam_init)
    o_ref[...] = (y * zs_ref[...]).astype(o_ref.dtype)


def _diff_attention(qkv, zs, rel_bias_t, lam_vecs, subln_g, lam_init):
    s_len = qkv.shape[0]
    t = ATT_T
    hw = DA_V_DIM
    nh = DA_HEADS
    grid = (nh, s_len // t)
    return pl.pallas_call(
        functools.partial(_attn_kernel, lam_init=lam_init),
        grid=grid,
        in_specs=[
            pl.BlockSpec(memory_space=pltpu.SMEM),
            pl.BlockSpec((t, hw), lambda h, i: (i, h)),
            pl.BlockSpec((s_len, hw), lambda h, i: (0, nh + h)),
            pl.BlockSpec((s_len, hw), lambda h, i: (0, 2 * nh + h)),
            pl.BlockSpec((t, hw), lambda h, i: (i, h)),
            pl.BlockSpec((4, DA_HEAD_DIM), lambda h, i: (0, 0)),
            pl.BlockSpec((1, hw), lambda h, i: (0, 0)),
        ],
        out_specs=pl.BlockSpec((t, hw), lambda h, i: (i, h)),
        out_shape=jax.ShapeDtypeStruct((s_len, nh * hw), BF16),
        scratch_shapes=[
            pltpu.VMEM((t, 2 * t), F32),
            pltpu.VMEM((2 * t, hw), F32),
            pltpu.VMEM((2 * t, LANE), F32), pltpu.VMEM((2 * t, LANE), F32),
        ],
        compiler_params=_cparams(("arbitrary", "arbitrary")),
        name="diff_attention",
    )(rel_bias_t, qkv, qkv, qkv, zs, lam_vecs, subln_g.reshape(1, hw))


def _sgu_kernel(gv_ref, guz_ref, lng_ref, lnb_ref, w_ref, bt_ref, o_ref):
    r = lax.broadcasted_iota(jnp.int32, (CHUNK, CHUNK), 0)
    c = lax.broadcasted_iota(jnp.int32, (CHUNK, CHUNK), 1)
    tril = r >= c
    gd = gv_ref.shape[1] // SG_GROUPS
    for ch in range(gv_ref.shape[0] // CHUNK):
        rows = slice(ch * CHUNK, (ch + 1) * CHUNK)
        v = gv_ref[rows, :]
        mu = jnp.mean(v, axis=-1, keepdims=True)
        xc = v - mu
        var = jnp.mean(xc * xc, axis=-1, keepdims=True)
        vn = ((xc * lax.rsqrt(var + EPS)) * lng_ref[...] + lnb_ref[...]).astype(BF16)
        for g in range(SG_GROUPS):
            sl = slice(g * gd, (g + 1) * gd)
            wg = jnp.where(tril, w_ref[g], jnp.zeros((CHUNK, CHUNK), BF16))
            mixed = jnp.dot(wg, vn[:, sl], preferred_element_type=F32) + bt_ref[:, g:g + 1]
            o_ref[rows, sl] = (guz_ref[rows, sl] * mixed).astype(o_ref.dtype)


def _spatial_gate(gv, guz, ln_g, ln_b, w_bf16, b_t, chunks_per_step=2):
    s_len, width = gv.shape
    rows = chunks_per_step * CHUNK
    row = pl.BlockSpec((rows, width), lambda i: (i, 0))
    vec = pl.BlockSpec((1, width), lambda i: (0, 0))
    return pl.pallas_call(
        _sgu_kernel,
        grid=(s_len // rows,),
        in_specs=[row, row, vec, vec,
                  pl.BlockSpec((SG_GROUPS, CHUNK, CHUNK), lambda i: (0, 0, 0)),
                  pl.BlockSpec((CHUNK, SG_GROUPS), lambda i: (0, 0))],
        out_specs=row,
        out_shape=jax.ShapeDtypeStruct((s_len, width), BF16),
        compiler_params=_cparams(("parallel",)),
        name="spatial_gate",
    )(gv, guz, ln_g.reshape(1, width), ln_b.reshape(1, width), w_bf16, b_t)


def _mem_kv_kernel(mem_ref, g_ref, w_ref, o_ref):
    x = mem_ref[...]
    ms = jnp.mean(x * x, axis=-1, keepdims=True)
    mn = ((x * lax.rsqrt(ms + EPS)) * g_ref[...]).astype(BF16)
    o_ref[...] = jnp.dot(mn, w_ref[...], preferred_element_type=F32).astype(o_ref.dtype)


def _mem_kv(mem, g, w_xkv):
    ml, d = mem.shape
    n = w_xkv.shape[1]
    return pl.pallas_call(
        _mem_kv_kernel,
        grid=(1,),
        in_specs=[pl.BlockSpec((ml, d), lambda i: (0, 0)),
                  pl.BlockSpec((1, d), lambda i: (0, 0)),
                  pl.BlockSpec((d, n), lambda i: (0, 0))],
        out_specs=pl.BlockSpec((ml, n), lambda i: (0, 0)),
        out_shape=jax.ShapeDtypeStruct((ml, n), BF16),
        compiler_params=_cparams(("arbitrary",)),
        name="mem_kv",
    )(mem, g.reshape(1, d), w_xkv)


def _xattn_kernel(h_ref, gx_ref, wq_ref, kv_ref, wo_ref, gf_ref, o_ref):
    hd = XA_HEAD_DIM
    xw = XA_HEADS * hd
    half = h_ref.shape[0] // 2
    parts = [slice(0, half), slice(half, 2 * half)]
    hs = [h_ref[r, :] for r in parts]
    hns = []
    for h in hs:
        ms = jnp.mean(h * h, axis=-1, keepdims=True)
        hns.append(((h * lax.rsqrt(ms + EPS)) * gx_ref[...]).astype(BF16))
    qs = [jnp.dot(hn, wq_ref[...], preferred_element_type=F32).astype(BF16) for hn in hns]
    os_ = []
    for q in qs:
        outs = []
        for a in range(XA_HEADS):
            k = kv_ref[:, a * hd:(a + 1) * hd]
            v = kv_ref[:, xw + a * hd:xw + (a + 1) * hd]
            s = lax.dot_general(q[:, a * hd:(a + 1) * hd], k, (((1,), (1,)), ((), ())),
                                preferred_element_type=F32) * (hd ** -0.5)
            p = jnp.exp(s - jnp.max(s, axis=-1, keepdims=True))
            p = p / jnp.sum(p, axis=-1, keepdims=True)
            outs.append(jnp.dot(p.astype(BF16), v, preferred_element_type=F32))
        os_.append(jnp.concatenate(outs, axis=-1).astype(BF16))
    h2s = [h + jnp.dot(o, wo_ref[...], preferred_element_type=F32) for h, o in zip(hs, os_)]
    outs = []
    for h2 in h2s:
        ms2 = jnp.mean(h2 * h2, axis=-1, keepdims=True)
        outs.append((h2 * lax.rsqrt(ms2 + EPS)) * gf_ref[...])
    o_ref[...] = jnp.concatenate(outs, axis=0)


def _xattn_final(h, gx, wq, kv, wo, gf, tm=512):
    m, d = h.shape
    row = pl.BlockSpec((tm, d), lambda i: (i, 0))
    vec = pl.BlockSpec((1, d), lambda i: (0, 0))
    full = lambda arr: pl.BlockSpec(arr.shape, lambda i: (0, 0),
                                    pipeline_mode=pl.Buffered(buffer_count=1))
    return pl.pallas_call(
        _xattn_kernel,
        grid=(m // tm,),
        in_specs=[row, vec, full(wq), full(kv), full(wo), vec],
        out_specs=row,
        out_shape=jax.ShapeDtypeStruct((m, d), F32),
        compiler_params=_cparams(("parallel",)),
        name="xattn_final",
    )(h, gx.reshape(1, d), wq, kv, wo, gf.reshape(1, d))


def kernel(x, positions, mem, w_in, b_gate, norm1_g, lam_q1, lam_k1, lam_q2, lam_k2,
           subln_g, rel_bias, sg_ln_g, sg_ln_b, sg_w, sg_b, w_proj_a, w_proj_b, w_out,
           norm_x_g, norm_mem_g, w_xq, w_xkv, w_xo, final_g):
    bsz, s_len, d = x.shape
    assert bsz == 1 and positions.shape == (bsz, s_len)
    depth = w_in.shape[0]
    assert depth == 1
    w = DA_HEADS * DA_V_DIM
    h = x.reshape(s_len, d)
    rel_bias_t = rel_bias.T
    for l in range(depth):
        lam_init = 0.8 - 0.6 * math.exp(-0.3 * l)
        hn = _rmsnorm(h, norm1_g[l], BF16)
        w_in_l = w_in[l]

        def in_proj(col0, ncols, epilogue, out_dtype, extras=()):
            return _wproj([hn], [w_in_l], [col0], ncols, epilogue, out_dtype, extras)

        qkv = in_proj(0, 3 * w, "qkv", BF16)
        zs_a = in_proj(3 * w, w, "silu", F32)
        gv = in_proj(5 * w, w, "gelu", F32)
        guz = _wproj([hn], [w_in_l, w_in_l], [4 * w, 6 * w], w, "gelu_silu_product", F32,
                     tn=512)
        tn = 1024
        gates = in_proj(7 * w, 2 * d, "sigmoid_bias", F32, extras=[
            (b_gate[l].reshape(1, 2 * d), (1, tn), lambda j, i: (0, j))])
        lam_vecs = jnp.stack([lam_q1[l], lam_k1[l], lam_q2[l], lam_k2[l]]).astype(F32)
        a_in = _diff_attention(qkv, zs_a, rel_bias_t, lam_vecs, subln_g[l], lam_init)
        b_in = _spatial_gate(gv, guz, sg_ln_g[l], sg_ln_b[l],
                             sg_w[l].astype(BF16), sg_b[l].T)
        tm, tn = 512, 512
        nb = d // tn
        merged = _wproj([a_in, b_in], [w_proj_a[l], w_proj_b[l]], [0, 0], d, "gated_sum", BF16,
                        extras=[(gates, (tm, tn), lambda j, i: (i, j)),
                                (gates, (tm, tn), lambda j, i, nb=nb: (i, j + nb))],
                        tm=tm, tn=tn)
        tm, tn = 1024, 512
        h = _wproj([merged], [w_out[l]], [0], d, "residual", F32,
                   extras=[(h, (tm, tn), lambda j, i: (i, j))], tm=tm, tn=tn)
        kv = _mem_kv(mem.reshape(-1, d), norm_mem_g[l], w_xkv[l].astype(BF16))
        h = _xattn_final(h, norm_x_g[l], w_xq[l].astype(BF16), kv, w_xo[l].astype(BF16), final_g)
    return h.reshape(bsz, s_len, d)
```

```python
import functools
import math

import jax
import jax.numpy as jnp
from jax import lax
from jax.experimental import pallas as pl
from jax.experimental.pallas import tpu as pltpu

F32 = jnp.float32
BF16 = jnp.bfloat16

EPS = 1e-6
LANE = 128
VMEM_LIMIT = 56 * 1024 * 1024

DA_HEADS = 16
DA_HEAD_DIM = 128
DA_V_DIM = 2 * DA_HEAD_DIM
SG_GROUPS = 16
CHUNK = 128
NUM_BUCKETS = 32
MAX_DISTANCE = 128
XA_HEADS = 4
XA_HEAD_DIM = 128

ATT_T = 512
SOFTMAX_C2 = DA_HEAD_DIM ** -0.5 * math.log2(math.e)
NEG = -1e30


def _cparams(sem):
    return pltpu.CompilerParams(dimension_semantics=sem, vmem_limit_bytes=VMEM_LIMIT)


def _sigmoid(x):
    return 0.5 * jnp.tanh(0.5 * x) + 0.5


def _silu(x):
    h = 0.5 * x
    return h * jnp.tanh(h) + h


def _gelu_tanh(x):
    c = math.sqrt(2.0 / math.pi)
    h = 0.5 * x
    return h * jnp.tanh(x * (c + (c * 0.044715) * (x * x))) + h


def _rmsnorm_kernel(x_ref, g_ref, o_ref):
    x = x_ref[...]
    ms = jnp.mean(x * x, axis=-1, keepdims=True)
    o_ref[...] = ((x * lax.rsqrt(ms + EPS)) * g_ref[...]).astype(o_ref.dtype)


def _rmsnorm(x, g, out_dtype, rows=512):
    m, d = x.shape
    return pl.pallas_call(
        _rmsnorm_kernel,
        grid=(m // rows,),
        in_specs=[pl.BlockSpec((rows, d), lambda i: (i, 0)),
                  pl.BlockSpec((1, d), lambda i: (0, 0))],
        out_specs=pl.BlockSpec((rows, d), lambda i: (i, 0)),
        out_shape=jax.ShapeDtypeStruct((m, d), out_dtype),
        compiler_params=_cparams(("parallel",)),
        name="rmsnorm",
    )(x, g.reshape(1, d))


def _epilogue(kind, ys, extras, col_tile):
    if kind == "cast":
        return ys[0]
    if kind == "qkv":
        q_tiles = DA_HEADS * DA_V_DIM // ys[0].shape[1]
        return ys[0] * jnp.where(col_tile < q_tiles, SOFTMAX_C2, 1.0)
    if kind == "silu":
        return _silu(ys[0])
    if kind == "gelu":
        return _gelu_tanh(ys[0])
    if kind == "sigmoid_bias":
        return _sigmoid(ys[0] + extras[0][...])
    if kind == "residual":
        return extras[0][...] + ys[0]
    if kind == "gated_sum":
        return extras[0][...] * ys[0] + extras[1][...] * ys[1]
    if kind == "gelu_silu_product":
        return _gelu_tanh(ys[0]) * _silu(ys[1])
    raise ValueError(kind)


def _wproj_kernel(*refs, n_a, n_w, n_extra, epilogue, kc, col_blk0):
    a_refs = refs[:n_a]
    refs = refs[n_a:]
    w_refs = refs[:n_w]
    extras = refs[n_w:n_w + n_extra]
    o_ref = refs[n_w + n_extra]
    wb_refs = refs[n_w + n_extra + 1:2 * n_w + n_extra + 1]
    st_refs = refs[2 * n_w + n_extra + 1:3 * n_w + n_extra + 1]
    sem = refs[-1]
    j = pl.program_id(0)
    i = pl.program_id(1)
    nj = pl.num_programs(0)
    ni = pl.num_programs(1)
    tn = o_ref.shape[1]

    def chunk_copy(w, jt, c, slot):
        return pltpu.make_async_copy(
            w_refs[w].at[pl.ds(c * kc, kc), pl.ds((col_blk0[w] + jt) * tn, tn)],
            st_refs[w].at[slot], sem.at[w, slot])

    def cast_chunk(w, c, slot, wslot):
        row0 = c * kc if isinstance(c, int) else pl.multiple_of(c * kc, kc)
        wb_refs[w][wslot, pl.ds(row0, kc), :] = st_refs[w][slot].astype(BF16)

    n_chunks = w_refs[0].shape[0] // kc

    @pl.when((j == 0) & (i == 0))
    def _first_tile():
        for w in range(n_w):
            chunk_copy(w, 0, 0, 0).start()
        for c in range(n_chunks):
            for w in range(n_w):
                if c + 1 < n_chunks:
                    chunk_copy(w, 0, c + 1, (c + 1) % 2).start()
                chunk_copy(w, 0, c, c % 2).wait()
                cast_chunk(w, c, c % 2, 0)
        for w in range(n_w):
            chunk_copy(w, 1 % nj, 0, 0).start()

    slot = i % 2
    cur = j % 2
    nxt = 1 - cur
    next_tile = (j + 1) % nj

    @pl.when(jnp.logical_not((j == nj - 1) & (i == ni - 1)))
    def _prefetch():
        wrap = i == ni - 1
        jt = jnp.where(wrap, (j + 2) % nj, next_tile)
        c = jnp.where(wrap, 0, i + 1)
        for w in range(n_w):
            chunk_copy(w, jt, c, 1 - slot).start()

    for w in range(n_w):
        chunk_copy(w, next_tile, i, slot).wait()
        cast_chunk(w, i, slot, nxt)

    ys = [jnp.dot(a_refs[w % n_a][...], wb_refs[w][cur], preferred_element_type=F32)
          for w in range(n_w)]
    o_ref[...] = _epilogue(epilogue, ys, extras, col_blk0[0] + j).astype(o_ref.dtype)


def _wproj(a_list, w_list, col0s, ncols, epilogue, out_dtype, extras=(), tm=1024, tn=1024):
    n_a, n_w = len(a_list), len(w_list)
    m, k = a_list[0].shape
    ni = m // tm
    assert n_a in (1, n_w) and len(col0s) == n_w and all(c % tn == 0 for c in col0s)
    assert ncols % tn == 0 and m % tm == 0 and ni % 2 == 0
    assert ncols // tn >= 2 and k % ni == 0
    kc = k // ni
    in_specs = ([pl.BlockSpec((tm, k), lambda j, i: (i, 0))] * n_a
                + [pl.BlockSpec(memory_space=pl.ANY)] * n_w
                + [pl.BlockSpec(bs, im) for _, bs, im in extras])
    return pl.pallas_call(
        functools.partial(_wproj_kernel, n_a=n_a, n_w=n_w, n_extra=len(extras),
                          epilogue=epilogue, kc=kc, col_blk0=tuple(c // tn for c in col0s)),
        grid=(ncols // tn, ni),
        in_specs=in_specs,
        out_specs=pl.BlockSpec((tm, tn), lambda j, i: (i, j)),
        out_shape=jax.ShapeDtypeStruct((m, ncols), out_dtype),
        scratch_shapes=([pltpu.VMEM((2, k, tn), BF16)] * n_w
                        + [pltpu.VMEM((2, kc, tn), F32)] * n_w
                        + [pltpu.SemaphoreType.DMA((n_w, 2))]),
        compiler_params=_cparams(("arbitrary", "arbitrary")),
        name="proj_" + epilogue,
    )(*a_list, *w_list, *[e[0] for e in extras])


def _t5_bucket(n):
    max_exact = NUM_BUCKETS // 2
    nf = jnp.maximum(n, 1).astype(F32)
    large = max_exact + (jnp.log(nf / max_exact) / math.log(MAX_DISTANCE / max_exact)
                         * (NUM_BUCKETS - max_exact)).astype(jnp.int32)
    large = jnp.minimum(large, NUM_BUCKETS - 1)
    return jnp.where(n < max_exact, n, large)


def _attn_kernel(rb_ref, q_ref, k_ref, v_ref, zs_ref, lam_ref, g_ref, o_ref,
                 dn_ref, corner_ref, acc_ref, m_ref, l_ref,
                 *, lam_init):
    t = ATT_T
    dh = DA_HEAD_DIM
    h = pl.program_id(0)
    i = pl.program_id(1)

    @pl.when(i == 0)
    def _build_bias():
        r = lax.broadcasted_iota(jnp.int32, (LANE, LANE), 0)
        c = lax.broadcasted_iota(jnp.int32, (LANE, LANE), 1)
        n0 = r - c
        b0 = _t5_bucket(jnp.maximum(n0, 0))
        b1 = _t5_bucket(n0 + LANE)
        last = rb_ref[h, NUM_BUCKETS - 1]
        p0 = jnp.zeros((LANE, LANE), F32)
        p1 = jnp.zeros((LANE, LANE), F32)
        for b in range(NUM_BUCKETS):
            val = (rb_ref[h, b] - last) * math.log2(math.e)
            p0 = jnp.where(b0 == b, val, p0)
            p1 = jnp.where(b1 == b, val, p1)
        p0 = jnp.where(n0 >= 0, p0, NEG)
        nblk = t // LANE
        zero = jnp.zeros((LANE, LANE), F32)
        neg = jnp.full((LANE, LANE), NEG, F32)
        for a in range(nblk):
            for b in range(nblk):
                blk = p0 if a == b else p1 if a == b + 1 else zero if a > b else neg
                dn_ref[a * LANE:(a + 1) * LANE, b * LANE:(b + 1) * LANE] = blk
        corner_ref[...] = p1

    def step(kb, width, near=False, first=False):
        tk = width * t
        off = pl.multiple_of(kb * t, t)
        k = k_ref[pl.ds(off, tk), :]
        v = v_ref[pl.ds(off, tk), :]
        ss = []
        for c in range(2):
            q = q_ref[:, c * dh:(c + 1) * dh]
            s = lax.dot_general(q, k[:, c * dh:(c + 1) * dh], (((1,), (1,)), ((), ())),
                                preferred_element_type=F32)
            pieces = []
            for j in range(tk // LANE):
                piece = s[:, j * LANE:(j + 1) * LANE]
                tile, col = divmod(j * LANE, t)
                if near and tile == width - 1:
                    piece = piece + dn_ref[:, col:col + LANE]
                elif near and tile == width - 2 and col == t - LANE:
                    piece = jnp.concatenate(
                        [piece[:LANE] + corner_ref[...], piece[LANE:]], axis=0)
                pieces.append(piece)
            ss.append(pieces)
        m_out, l_out, alphas, probs = [], [], [], []
        for c in range(2):
            rows = slice(c * t, (c + 1) * t)
            m_cur = functools.reduce(jnp.maximum, ss[c])
            m_row = jnp.max(m_cur, axis=1, keepdims=True)
            if first:
                m_new = jnp.broadcast_to(m_row, (t, LANE))
                alpha = None
            else:
                m_old = m_ref[rows, :]
                m_new = jnp.maximum(m_old, m_row)
                alpha = jnp.exp2(m_old - m_new)
            ps = [jnp.exp2(piece - m_new) for piece in ss[c]]
            l_new = functools.reduce(lambda a, b: a + b, ps)
            l_out.append(l_new if first else alpha * l_ref[rows, :] + l_new)
            probs.append(jnp.concatenate(ps, axis=1).astype(BF16))
            alphas.append(alpha)
            m_out.append(m_new)
        acc_out = []
        for c in range(2):
            rows = slice(c * t, (c + 1) * t)
            pv = jnp.dot(probs[c], v, preferred_element_type=F32)
            if first:
                acc_out.append(pv)
            else:
                alpha_w = jnp.concatenate([alphas[c]] * (acc_ref.shape[1] // LANE), axis=1)
                acc_out.append(alpha_w * acc_ref[rows, :] + pv)
        m_ref[...] = jnp.concatenate(m_out, axis=0)
        l_ref[...] = jnp.concatenate(l_out, axis=0)
        acc_ref[...] = jnp.concatenate(acc_out, axis=0)

    n_far = jnp.maximum(i - 1, 0)
    n_quads = n_far // 4

    @pl.when(n_quads >= 1)
    def _first_quad():
        step(0, 4, first=True)

    def far_quad(j, carry):
        step(4 * j, 4)
        return carry

    lax.fori_loop(1, n_quads, far_quad, 0)

    for rem in range(4):
        for first in (True, False):
            @pl.when((i >= 1) & (n_far % 4 == rem) & ((n_quads == 0) == first))
            def _tail(rem=rem, first=first):
                step(n_far - rem, rem + 2, near=True, first=first)

    @pl.when(i == 0)
    def _diagonal_only():
        step(0, 1, near=True, first=True)

    lam_v = lam_ref[...]
    lam = (jnp.exp(jnp.sum(lam_v[0:1] * lam_v[1:2], axis=1, keepdims=True))
           - jnp.exp(jnp.sum(lam_v[2:3] * lam_v[3:4], axis=1, keepdims=True)) + lam_init)
    l_sum = jnp.sum(l_ref[...], axis=1, keepdims=True)
    o = acc_ref[0:t, :] / l_sum[0:t] - lam * (acc_ref[t:2 * t, :] / l_sum[t:2 * t])
    ms = jnp.mean(o * o, axis=-1, keepdims=True)
    y = ((o * lax.rsqrt(ms + EPS)) * g_ref[...]) * (1.0 - lam_init)
    o_ref[...] = (y * zs_ref[...]).astype(o_ref.dtype)


def _diff_attention(qkv, zs, rel_bias_t, lam_vecs, subln_g, lam_init):
    s_len = qkv.shape[0]
    t = ATT_T
    hw = DA_V_DIM
    nh = DA_HEADS
    grid = (nh, s_len // t)
    return pl.pallas_call(
        functools.partial(_attn_kernel, lam_init=lam_init),
        grid=grid,
        in_specs=[
            pl.BlockSpec(memory_space=pltpu.SMEM),
            pl.BlockSpec((t, hw), lambda h, i: (i, h)),
            pl.BlockSpec((s_len, hw), lambda h, i: (0, nh + h)),
            pl.BlockSpec((s_len, hw), lambda h, i: (0, 2 * nh + h)),
            pl.BlockSpec((t, hw), lambda h, i: (i, h)),
            pl.BlockSpec((4, DA_HEAD_DIM), lambda h, i: (0, 0)),
            pl.BlockSpec((1, hw), lambda h, i: (0, 0)),
        ],
        out_specs=pl.BlockSpec((t, hw), lambda h, i: (i, h)),
        out_shape=jax.ShapeDtypeStruct((s_len, nh * hw), BF16),
        scratch_shapes=[
            pltpu.VMEM((t, t), F32), pltpu.VMEM((LANE, LANE), F32),
            pltpu.VMEM((2 * t, hw), F32),
            pltpu.VMEM((2 * t, LANE), F32), pltpu.VMEM((2 * t, LANE), F32),
        ],
        compiler_params=_cparams(("arbitrary", "arbitrary")),
        name="diff_attention",
    )(rel_bias_t, qkv, qkv, qkv, zs, lam_vecs, subln_g.reshape(1, hw))


def _sgu_kernel(gv_ref, guz_ref, lng_ref, lnb_ref, w_ref, bt_ref, o_ref):
    r = lax.broadcasted_iota(jnp.int32, (CHUNK, CHUNK), 0)
    c = lax.broadcasted_iota(jnp.int32, (CHUNK, CHUNK), 1)
    tril = r >= c
    gd = gv_ref.shape[1] // SG_GROUPS
    for ch in range(gv_ref.shape[0] // CHUNK):
        rows = slice(ch * CHUNK, (ch + 1) * CHUNK)
        v = gv_ref[rows, :]
        mu = jnp.mean(v, axis=-1, keepdims=True)
        xc = v - mu
        var = jnp.mean(xc * xc, axis=-1, keepdims=True)
        vn = ((xc * lax.rsqrt(var + EPS)) * lng_ref[...] + lnb_ref[...]).astype(BF16)
        for g in range(SG_GROUPS):
            sl = slice(g * gd, (g + 1) * gd)
            wg = jnp.where(tril, w_ref[g], jnp.zeros((CHUNK, CHUNK), BF16))
            mixed = jnp.dot(wg, vn[:, sl], preferred_element_type=F32) + bt_ref[:, g:g + 1]
            o_ref[rows, sl] = (guz_ref[rows, sl] * mixed).astype(o_ref.dtype)


def _spatial_gate(gv, guz, ln_g, ln_b, w_bf16, b_t, chunks_per_step=2):
    s_len, width = gv.shape
    rows = chunks_per_step * CHUNK
    row = pl.BlockSpec((rows, width), lambda i: (i, 0))
    vec = pl.BlockSpec((1, width), lambda i: (0, 0))
    return pl.pallas_call(
        _sgu_kernel,
        grid=(s_len // rows,),
        in_specs=[row, row, vec, vec,
                  pl.BlockSpec((SG_GROUPS, CHUNK, CHUNK), lambda i: (0, 0, 0)),
                  pl.BlockSpec((CHUNK, SG_GROUPS), lambda i: (0, 0))],
        out_specs=row,
        out_shape=jax.ShapeDtypeStruct((s_len, width), BF16),
        compiler_params=_cparams(("parallel",)),
        name="spatial_gate",
    )(gv, guz, ln_g.reshape(1, width), ln_b.reshape(1, width), w_bf16, b_t)


def _mem_kv_kernel(mem_ref, g_ref, w_ref, o_ref):
    x = mem_ref[...]
    ms = jnp.mean(x * x, axis=-1, keepdims=True)
    mn = ((x * lax.rsqrt(ms + EPS)) * g_ref[...]).astype(BF16)
    o_ref[...] = jnp.dot(mn, w_ref[...], preferred_element_type=F32).astype(o_ref.dtype)


def _mem_kv(mem, g, w_xkv):
    ml, d = mem.shape
    n = w_xkv.shape[1]
    return pl.pallas_call(
        _mem_kv_kernel,
        grid=(1,),
        in_specs=[pl.BlockSpec((ml, d), lambda i: (0, 0)),
                  pl.BlockSpec((1, d), lambda i: (0, 0)),
                  pl.BlockSpec((d, n), lambda i: (0, 0))],
        out_specs=pl.BlockSpec((ml, n), lambda i: (0, 0)),
        out_shape=jax.ShapeDtypeStruct((ml, n), BF16),
        compiler_params=_cparams(("arbitrary",)),
        name="mem_kv",
    )(mem, g.reshape(1, d), w_xkv)


def _xattn_kernel(h_ref, gx_ref, wq_ref, kv_ref, wo_ref, gf_ref, o_ref):
    hd = XA_HEAD_DIM
    xw = XA_HEADS * hd
    half = h_ref.shape[0] // 2
    parts = [slice(0, half), slice(half, 2 * half)]
    hs = [h_ref[r, :] for r in parts]
    hns = []
    for h in hs:
        ms = jnp.mean(h * h, axis=-1, keepdims=True)
        hns.append(((h * lax.rsqrt(ms + EPS)) * gx_ref[...]).astype(BF16))
    qs = [jnp.dot(hn, wq_ref[...], preferred_element_type=F32).astype(BF16) for hn in hns]
    os_ = []
    for q in qs:
        outs = []
        for a in range(XA_HEADS):
            k = kv_ref[:, a * hd:(a + 1) * hd]
            v = kv_ref[:, xw + a * hd:xw + (a + 1) * hd]
            s = lax.dot_general(q[:, a * hd:(a + 1) * hd], k, (((1,), (1,)), ((), ())),
                                preferred_element_type=F32) * (hd ** -0.5)
            p = jnp.exp(s - jnp.max(s, axis=-1, keepdims=True))
            p = p / jnp.sum(p, axis=-1, keepdims=True)
            outs.append(jnp.dot(p.astype(BF16), v, preferred_element_type=F32))
        os_.append(jnp.concatenate(outs, axis=-1).astype(BF16))
    h2s = [h + jnp.dot(o, wo_ref[...], preferred_element_type=F32) for h, o in zip(hs, os_)]
    outs = []
    for h2 in h2s:
        ms2 = jnp.mean(h2 * h2, axis=-1, keepdims=True)
        outs.append((h2 * lax.rsqrt(ms2 + EPS)) * gf_ref[...])
    o_ref[...] = jnp.concatenate(outs, axis=0)


def _xattn_final(h, gx, wq, kv, wo, gf, tm=512):
    m, d = h.shape
    row = pl.BlockSpec((tm, d), lambda i: (i, 0))
    vec = pl.BlockSpec((1, d), lambda i: (0, 0))
    full = lambda arr: pl.BlockSpec(arr.shape, lambda i: (0, 0),
                                    pipeline_mode=pl.Buffered(buffer_count=1))
    return pl.pallas_call(
        _xattn_kernel,
        grid=(m // tm,),
        in_specs=[row, vec, full(wq), full(kv), full(wo), vec],
        out_specs=row,
        out_shape=jax.ShapeDtypeStruct((m, d), F32),
        compiler_params=_cparams(("parallel",)),
        name="xattn_final",
    )(h, gx.reshape(1, d), wq, kv, wo, gf.reshape(1, d))


def kernel(x, positions, mem, w_in, b_gate, norm1_g, lam_q1, lam_k1, lam_q2, lam_k2,
           subln_g, rel_bias, sg_ln_g, sg_ln_b, sg_w, sg_b, w_proj_a, w_proj_b, w_out,
           norm_x_g, norm_mem_g, w_xq, w_xkv, w_xo, final_g):
    bsz, s_len, d = x.shape
    assert bsz == 1 and positions.shape == (bsz, s_len)
    depth = w_in.shape[0]
    assert depth == 1
    w = DA_HEADS * DA_V_DIM
    h = x.reshape(s_len, d)
    rel_bias_t = rel_bias.T
    for l in range(depth):
        lam_init = 0.8 - 0.6 * math.exp(-0.3 * l)
        hn = _rmsnorm(h, norm1_g[l], BF16)
        w_in_l = w_in[l]

        def in_proj(col0, ncols, epilogue, out_dtype, extras=()):
            return _wproj([hn], [w_in_l], [col0], ncols, epilogue, out_dtype, extras)

        qkv = in_proj(0, 3 * w, "qkv", BF16)
        zs_a = in_proj(3 * w, w, "silu", F32)
        gv = in_proj(5 * w, w, "gelu", F32)
        guz = _wproj([hn], [w_in_l, w_in_l], [4 * w, 6 * w], w, "gelu_silu_product", F32,
                     tn=512)
        tn = 1024
        gates = in_proj(7 * w, 2 * d, "sigmoid_bias", F32, extras=[
            (b_gate[l].reshape(1, 2 * d), (1, tn), lambda j, i: (0, j))])
        lam_vecs = jnp.stack([lam_q1[l], lam_k1[l], lam_q2[l], lam_k2[l]]).astype(F32)
        a_in = _diff_attention(qkv, zs_a, rel_bias_t, lam_vecs, subln_g[l], lam_init)
        b_in = _spatial_gate(gv, guz, sg_ln_g[l], sg_ln_b[l],
                             sg_w[l].astype(BF16), sg_b[l].T)
        tm, tn = 512, 512
        nb = d // tn
        merged = _wproj([a_in, b_in], [w_proj_a[l], w_proj_b[l]], [0, 0], d, "gated_sum", BF16,
                        extras=[(gates, (tm, tn), lambda j, i: (i, j)),
                                (gates, (tm, tn), lambda j, i, nb=nb: (i, j + nb))],
                        tm=tm, tn=tn)
        tm, tn = 1024, 512
        h = _wproj([merged], [w_out[l]], [0], d, "residual", F32,
                   extras=[(h, (tm, tn), lambda j, i: (i, j))], tm=tm, tn=tn)
        kv = _mem_kv(mem.reshape(-1, d), norm_mem_g[l], w_xkv[l].astype(BF16))
        h = _xattn_final(h, norm_x_g[l], w_xq[l].astype(BF16), kv, w_xo[l].astype(BF16), final_g)
    return h.reshape(bsz, s_len, d)
```

```python
import functools
import math

import jax
import jax.numpy as jnp
from jax import lax
from jax.experimental import pallas as pl
from jax.experimental.pallas import tpu as pltpu

F32 = jnp.float32
BF16 = jnp.bfloat16

EPS = 1e-6
LANE = 128
VMEM_LIMIT = 56 * 1024 * 1024

DA_HEADS = 16
DA_HEAD_DIM = 128
DA_V_DIM = 2 * DA_HEAD_DIM
SG_GROUPS = 16
CHUNK = 128
NUM_BUCKETS = 32
MAX_DISTANCE = 128
XA_HEADS = 4
XA_HEAD_DIM = 128

ATT_T = 512
SOFTMAX_C2 = DA_HEAD_DIM ** -0.5 * math.log2(math.e)
NEG = -1e30


def _cparams(sem):
    return pltpu.CompilerParams(dimension_semantics=sem, vmem_limit_bytes=VMEM_LIMIT)


def _sigmoid(x):
    return 0.5 * jnp.tanh(0.5 * x) + 0.5


def _silu(x):
    h = 0.5 * x
    return h * jnp.tanh(h) + h


def _gelu_tanh(x):
    c = math.sqrt(2.0 / math.pi)
    h = 0.5 * x
    return h * jnp.tanh(x * (c + (c * 0.044715) * (x * x))) + h


def _rmsnorm_kernel(x_ref, g_ref, o_ref):
    x = x_ref[...]
    ms = jnp.mean(x * x, axis=-1, keepdims=True)
    o_ref[...] = ((x * lax.rsqrt(ms + EPS)) * g_ref[...]).astype(o_ref.dtype)


def _rmsnorm(x, g, out_dtype, rows=512):
    m, d = x.shape
    return pl.pallas_call(
        _rmsnorm_kernel,
        grid=(m // rows,),
        in_specs=[pl.BlockSpec((rows, d), lambda i: (i, 0)),
                  pl.BlockSpec((1, d), lambda i: (0, 0))],
        out_specs=pl.BlockSpec((rows, d), lambda i: (i, 0)),
        out_shape=jax.ShapeDtypeStruct((m, d), out_dtype),
        compiler_params=_cparams(("parallel",)),
        name="rmsnorm",
    )(x, g.reshape(1, d))


def _epilogue(kind, ys, extras, col_tile):
    if kind == "cast":
        return ys[0]
    if kind == "qkv":
        q_tiles = DA_HEADS * DA_V_DIM // ys[0].shape[1]
        return ys[0] * jnp.where(col_tile < q_tiles, SOFTMAX_C2, 1.0)
    if kind == "silu":
        return _silu(ys[0])
    if kind == "gelu":
        return _gelu_tanh(ys[0])
    if kind == "sigmoid_bias":
        return _sigmoid(ys[0] + extras[0][...])
    if kind == "residual":
        return extras[0][...] + ys[0]
    if kind == "gated_sum":
        return extras[0][...] * ys[0] + extras[1][...] * ys[1]
    if kind == "gelu_silu_product":
        return _gelu_tanh(ys[0]) * _silu(ys[1])
    raise ValueError(kind)


def _wproj_kernel(*refs, n_a, n_w, n_extra, epilogue, kc, col_blk0):
    a_refs = refs[:n_a]
    refs = refs[n_a:]
    w_refs = refs[:n_w]
    extras = refs[n_w:n_w + n_extra]
    o_ref = refs[n_w + n_extra]
    wb_refs = refs[n_w + n_extra + 1:2 * n_w + n_extra + 1]
    st_refs = refs[2 * n_w + n_extra + 1:3 * n_w + n_extra + 1]
    sem = refs[-1]
    j = pl.program_id(0)
    i = pl.program_id(1)
    nj = pl.num_programs(0)
    ni = pl.num_programs(1)
    tn = o_ref.shape[1]

    def chunk_copy(w, jt, c, slot):
        return pltpu.make_async_copy(
            w_refs[w].at[pl.ds(c * kc, kc), pl.ds((col_blk0[w] + jt) * tn, tn)],
            st_refs[w].at[slot], sem.at[w, slot])

    def cast_chunk(w, c, slot, wslot):
        row0 = c * kc if isinstance(c, int) else pl.multiple_of(c * kc, kc)
        wb_refs[w][wslot, pl.ds(row0, kc), :] = st_refs[w][slot].astype(BF16)

    n_chunks = w_refs[0].shape[0] // kc

    @pl.when((j == 0) & (i == 0))
    def _first_tile():
        for w in range(n_w):
            chunk_copy(w, 0, 0, 0).start()
        for c in range(n_chunks):
            for w in range(n_w):
                if c + 1 < n_chunks:
                    chunk_copy(w, 0, c + 1, (c + 1) % 2).start()
                chunk_copy(w, 0, c, c % 2).wait()
                cast_chunk(w, c, c % 2, 0)
        for w in range(n_w):
            chunk_copy(w, 1 % nj, 0, 0).start()

    slot = i % 2
    cur = j % 2
    nxt = 1 - cur
    next_tile = (j + 1) % nj

    @pl.when(jnp.logical_not((j == nj - 1) & (i == ni - 1)))
    def _prefetch():
        wrap = i == ni - 1
        jt = jnp.where(wrap, (j + 2) % nj, next_tile)
        c = jnp.where(wrap, 0, i + 1)
        for w in range(n_w):
            chunk_copy(w, jt, c, 1 - slot).start()

    for w in range(n_w):
        chunk_copy(w, next_tile, i, slot).wait()
        cast_chunk(w, i, slot, nxt)

    ys = [jnp.dot(a_refs[w % n_a][...], wb_refs[w][cur], preferred_element_type=F32)
          for w in range(n_w)]
    o_ref[...] = _epilogue(epilogue, ys, extras, col_blk0[0] + j).astype(o_ref.dtype)


def _wproj(a_list, w_list, col0s, ncols, epilogue, out_dtype, extras=(), tm=1024, tn=1024):
    n_a, n_w = len(a_list), len(w_list)
    m, k = a_list[0].shape
    ni = m // tm
    assert n_a in (1, n_w) and len(col0s) == n_w and all(c % tn == 0 for c in col0s)
    assert ncols % tn == 0 and m % tm == 0 and ni % 2 == 0
    assert ncols // tn >= 2 and k % ni == 0
    kc = k // ni
    in_specs = ([pl.BlockSpec((tm, k), lambda j, i: (i, 0))] * n_a
                + [pl.BlockSpec(memory_space=pl.ANY)] * n_w
                + [pl.BlockSpec(bs, im) for _, bs, im in extras])
    return pl.pallas_call(
        functools.partial(_wproj_kernel, n_a=n_a, n_w=n_w, n_extra=len(extras),
                          epilogue=epilogue, kc=kc, col_blk0=tuple(c // tn for c in col0s)),
        grid=(ncols // tn, ni),
        in_specs=in_specs,
        out_specs=pl.BlockSpec((tm, tn), lambda j, i: (i, j)),
        out_shape=jax.ShapeDtypeStruct((m, ncols), out_dtype),
        scratch_shapes=([pltpu.VMEM((2, k, tn), BF16)] * n_w
                        + [pltpu.VMEM((2, kc, tn), F32)] * n_w
                        + [pltpu.SemaphoreType.DMA((n_w, 2))]),
        compiler_params=_cparams(("arbitrary", "arbitrary")),
        name="proj_" + epilogue,
    )(*a_list, *w_list, *[e[0] for e in extras])


def _t5_bucket(n):
    max_exact = NUM_BUCKETS // 2
    nf = jnp.maximum(n, 1).astype(F32)
    large = max_exact + (jnp.log(nf / max_exact) / math.log(MAX_DISTANCE / max_exact)
                         * (NUM_BUCKETS - max_exact)).astype(jnp.int32)
    large = jnp.minimum(large, NUM_BUCKETS - 1)
    return jnp.where(n < max_exact, n, large)


def _attn_kernel(rb_ref, q_ref, k_ref, v_ref, zs_ref, lam_ref, g_ref, o_ref,
                 dn_ref, corner_ref, acc_ref, m_ref, l_ref,
                 *, lam_init):
    t = ATT_T
    dh = DA_HEAD_DIM
    h = pl.program_id(0)
    i = pl.program_id(1)

    @pl.when(i == 0)
    def _build_bias():
        r = lax.broadcasted_iota(jnp.int32, (LANE, LANE), 0)
        c = lax.broadcasted_iota(jnp.int32, (LANE, LANE), 1)
        n0 = r - c
        b0 = _t5_bucket(jnp.maximum(n0, 0))
        b1 = _t5_bucket(n0 + LANE)
        last = rb_ref[h, NUM_BUCKETS - 1]
        p0 = jnp.zeros((LANE, LANE), F32)
        p1 = jnp.zeros((LANE, LANE), F32)
        for b in range(NUM_BUCKETS):
            val = (rb_ref[h, b] - last) * math.log2(math.e)
            p0 = jnp.where(b0 == b, val, p0)
            p1 = jnp.where(b1 == b, val, p1)
        p0 = jnp.where(n0 >= 0, p0, NEG)
        nblk = t // LANE
        zero = jnp.zeros((LANE, LANE), F32)
        neg = jnp.full((LANE, LANE), NEG, F32)
        for a in range(nblk):
            for b in range(nblk):
                blk = p0 if a == b else p1 if a == b + 1 else zero if a > b else neg
                dn_ref[a * LANE:(a + 1) * LANE, b * LANE:(b + 1) * LANE] = blk
        corner_ref[...] = p1

    def step(kb, width, near=False, first=False):
        tk = width * t
        off = pl.multiple_of(kb * t, t)
        k = k_ref[pl.ds(off, tk), :]
        v = v_ref[pl.ds(off, tk), :]
        ss = []
        for c in range(2):
            q = q_ref[:, c * dh:(c + 1) * dh]
            s = lax.dot_general(q, k[:, c * dh:(c + 1) * dh], (((1,), (1,)), ((), ())),
                                preferred_element_type=F32)
            pieces = []
            for j in range(tk // LANE):
                piece = s[:, j * LANE:(j + 1) * LANE]
                tile, col = divmod(j * LANE, t)
                if near and tile == width - 1:
                    piece = piece + dn_ref[:, col:col + LANE]
                elif near and tile == width - 2 and col == t - LANE:
                    piece = jnp.concatenate(
                        [piece[:LANE] + corner_ref[...], piece[LANE:]], axis=0)
                pieces.append(piece)
            ss.append(pieces)
        m_out, l_out, alphas, probs = [], [], [], []
        for c in range(2):
            rows = slice(c * t, (c + 1) * t)
            m_cur = functools.reduce(jnp.maximum, ss[c])
            m_row = jnp.max(m_cur, axis=1, keepdims=True)
            if first:
                m_new = jnp.broadcast_to(m_row, (t, LANE))
                alpha = None
            else:
                m_old = m_ref[rows, :]
                m_new = jnp.maximum(m_old, m_row)
                alpha = jnp.exp2(m_old - m_new)
            ps = [jnp.exp2(piece - m_new) for piece in ss[c]]
            l_new = functools.reduce(lambda a, b: a + b, ps)
            l_out.append(l_new if first else alpha * l_ref[rows, :] + l_new)
            probs.append(jnp.concatenate(ps, axis=1).astype(BF16))
            alphas.append(alpha)
            m_out.append(m_new)
        acc_out = []
        for c in range(2):
            rows = slice(c * t, (c + 1) * t)
            pv = jnp.dot(probs[c], v, preferred_element_type=F32)
            if first:
                acc_out.append(pv)
            else:
                alpha_w = jnp.concatenate([alphas[c]] * (acc_ref.shape[1] // LANE), axis=1)
                acc_out.append(alpha_w * acc_ref[rows, :] + pv)
        m_ref[...] = jnp.concatenate(m_out, axis=0)
        l_ref[...] = jnp.concatenate(l_out, axis=0)
        acc_ref[...] = jnp.concatenate(acc_out, axis=0)

    n_far = jnp.maximum(i - 1, 0)
    n_hex = n_far // 6
    left = n_far - 6 * n_hex
    has_quad = left >= 4
    rem = jnp.where(has_quad, left - 4, left)

    @pl.when(n_hex >= 1)
    def _first_hex():
        step(0, 6, first=True)

    def far_hex(j, carry):
        step(6 * j, 6)
        return carry

    lax.fori_loop(1, n_hex, far_hex, 0)

    for first in (True, False):
        @pl.when(has_quad & ((n_hex == 0) == first))
        def _far_quad(first=first):
            step(6 * n_hex, 4, first=first)

    for tail_rem in range(4):
        for first in (True, False):
            @pl.when((i >= 1) & (rem == tail_rem) & ((n_far < 4) == first))
            def _tail(tail_rem=tail_rem, first=first):
                step(n_far - tail_rem, tail_rem + 2, near=True, first=first)

    @pl.when(i == 0)
    def _diagonal_only():
        step(0, 1, near=True, first=True)

    lam_v = lam_ref[...]
    lam = (jnp.exp(jnp.sum(lam_v[0:1] * lam_v[1:2], axis=1, keepdims=True))
           - jnp.exp(jnp.sum(lam_v[2:3] * lam_v[3:4], axis=1, keepdims=True)) + lam_init)
    l_sum = jnp.sum(l_ref[...], axis=1, keepdims=True)
    o = acc_ref[0:t, :] / l_sum[0:t] - lam * (acc_ref[t:2 * t, :] / l_sum[t:2 * t])
    ms = jnp.mean(o * o, axis=-1, keepdims=True)
    y = ((o * lax.rsqrt(ms + EPS)) * g_ref[...]) * (1.0 - lam_init)
    o_ref[...] = (y * zs_ref[...]).astype(o_ref.dtype)


def _diff_attention(qkv, zs, rel_bias_t, lam_vecs, subln_g, lam_init):
    s_len = qkv.shape[0]
    t = ATT_T
    hw = DA_V_DIM
    nh = DA_HEADS
    grid = (nh, s_len // t)
    return pl.pallas_call(
        functools.partial(_attn_kernel, lam_init=lam_init),
        grid=grid,
        in_specs=[
            pl.BlockSpec(memory_space=pltpu.SMEM),
            pl.BlockSpec((t, hw), lambda h, i: (i, h)),
            pl.BlockSpec((s_len, hw), lambda h, i: (0, nh + h)),
            pl.BlockSpec((s_len, hw), lambda h, i: (0, 2 * nh + h)),
            pl.BlockSpec((t, hw), lambda h, i: (i, h)),
            pl.BlockSpec((4, DA_HEAD_DIM), lambda h, i: (0, 0)),
            pl.BlockSpec((1, hw), lambda h, i: (0, 0)),
        ],
        out_specs=pl.BlockSpec((t, hw), lambda h, i: (i, h)),
        out_shape=jax.ShapeDtypeStruct((s_len, nh * hw), BF16),
        scratch_shapes=[
            pltpu.VMEM((t, t), F32), pltpu.VMEM((LANE, LANE), F32),
            pltpu.VMEM((2 * t, hw), F32),
            pltpu.VMEM((2 * t, LANE), F32), pltpu.VMEM((2 * t, LANE), F32),
        ],
        compiler_params=_cparams(("arbitrary", "arbitrary")),
        name="diff_attention",
    )(rel_bias_t, qkv, qkv, qkv, zs, lam_vecs, subln_g.reshape(1, hw))


def _sgu_kernel(gv_ref, guz_ref, lng_ref, lnb_ref, w_ref, bt_ref, o_ref):
    r = lax.broadcasted_iota(jnp.int32, (CHUNK, CHUNK), 0)
    c = lax.broadcasted_iota(jnp.int32, (CHUNK, CHUNK), 1)
    tril = r >= c
    gd = gv_ref.shape[1] // SG_GROUPS
    for ch in range(gv_ref.shape[0] // CHUNK):
        rows = slice(ch * CHUNK, (ch + 1) * CHUNK)
        v = gv_ref[rows, :]
        mu = jnp.mean(v, axis=-1, keepdims=True)
        xc = v - mu
        var = jnp.mean(xc * xc, axis=-1, keepdims=True)
        vn = ((xc * lax.rsqrt(var + EPS)) * lng_ref[...] + lnb_ref[...]).astype(BF16)
        for g in range(SG_GROUPS):
            sl = slice(g * gd, (g + 1) * gd)
            wg = jnp.where(tril, w_ref[g], jnp.zeros((CHUNK, CHUNK), BF16))
            mixed = jnp.dot(wg, vn[:, sl], preferred_element_type=F32) + bt_ref[:, g:g + 1]
            o_ref[rows, sl] = (guz_ref[rows, sl] * mixed).astype(o_ref.dtype)


def _spatial_gate(gv, guz, ln_g, ln_b, w_bf16, b_t, chunks_per_step=2):
    s_len, width = gv.shape
    rows = chunks_per_step * CHUNK
    row = pl.BlockSpec((rows, width), lambda i: (i, 0))
    vec = pl.BlockSpec((1, width), lambda i: (0, 0))
    return pl.pallas_call(
        _sgu_kernel,
        grid=(s_len // rows,),
        in_specs=[row, row, vec, vec,
                  pl.BlockSpec((SG_GROUPS, CHUNK, CHUNK), lambda i: (0, 0, 0)),
                  pl.BlockSpec((CHUNK, SG_GROUPS), lambda i: (0, 0))],
        out_specs=row,
        out_shape=jax.ShapeDtypeStruct((s_len, width), BF16),
        compiler_params=_cparams(("parallel",)),
        name="spatial_gate",
    )(gv, guz, ln_g.reshape(1, width), ln_b.reshape(1, width), w_bf16, b_t)


def _mem_kv_kernel(mem_ref, g_ref, w_ref, o_ref):
    x = mem_ref[...]
    ms = jnp.mean(x * x, axis=-1, keepdims=True)
    mn = ((x * lax.rsqrt(ms + EPS)) * g_ref[...]).astype(BF16)
    o_ref[...] = jnp.dot(mn, w_ref[...], preferred_element_type=F32).astype(o_ref.dtype)


def _mem_kv(mem, g, w_xkv):
    ml, d = mem.shape
    n = w_xkv.shape[1]
    return pl.pallas_call(
        _mem_kv_kernel,
        grid=(1,),
        in_specs=[pl.BlockSpec((ml, d), lambda i: (0, 0)),
                  pl.BlockSpec((1, d), lambda i: (0, 0)),
                  pl.BlockSpec((d, n), lambda i: (0, 0))],
        out_specs=pl.BlockSpec((ml, n), lambda i: (0, 0)),
        out_shape=jax.ShapeDtypeStruct((ml, n), BF16),
        compiler_params=_cparams(("arbitrary",)),
        name="mem_kv",
    )(mem, g.reshape(1, d), w_xkv)


def _xattn_kernel(h_ref, gx_ref, wq_ref, kv_ref, wo_ref, gf_ref, o_ref):
    hd = XA_HEAD_DIM
    xw = XA_HEADS * hd
    half = h_ref.shape[0] // 2
    parts = [slice(0, half), slice(half, 2 * half)]
    hs = [h_ref[r, :] for r in parts]
    hns = []
    for h in hs:
        ms = jnp.mean(h * h, axis=-1, keepdims=True)
        hns.append(((h * lax.rsqrt(ms + EPS)) * gx_ref[...]).astype(BF16))
    qs = [jnp.dot(hn, wq_ref[...], preferred_element_type=F32).astype(BF16) for hn in hns]
    os_ = []
    for q in qs:
        outs = []
        for a in range(XA_HEADS):
            k = kv_ref[:, a * hd:(a + 1) * hd]
            v = kv_ref[:, xw + a * hd:xw + (a + 1) * hd]
            s = lax.dot_general(q[:, a * hd:(a + 1) * hd], k, (((1,), (1,)), ((), ())),
                                preferred_element_type=F32) * (hd ** -0.5)
            p = jnp.exp(s - jnp.max(s, axis=-1, keepdims=True))
            p = p / jnp.sum(p, axis=-1, keepdims=True)
            outs.append(jnp.dot(p.astype(BF16), v, preferred_element_type=F32))
        os_.append(jnp.concatenate(outs, axis=-1).astype(BF16))
    h2s = [h + jnp.dot(o, wo_ref[...], preferred_element_type=F32) for h, o in zip(hs, os_)]
    outs = []
    for h2 in h2s:
        ms2 = jnp.mean(h2 * h2, axis=-1, keepdims=True)
        outs.append((h2 * lax.rsqrt(ms2 + EPS)) * gf_ref[...])
    o_ref[...] = jnp.concatenate(outs, axis=0)


def _xattn_final(h, gx, wq, kv, wo, gf, tm=512):
    m, d = h.shape
    row = pl.BlockSpec((tm, d), lambda i: (i, 0))
    vec = pl.BlockSpec((1, d), lambda i: (0, 0))
    full = lambda arr: pl.BlockSpec(arr.shape, lambda i: (0, 0),
                                    pipeline_mode=pl.Buffered(buffer_count=1))
    return pl.pallas_call(
        _xattn_kernel,
        grid=(m // tm,),
        in_specs=[row, vec, full(wq), full(kv), full(wo), vec],
        out_specs=row,
        out_shape=jax.ShapeDtypeStruct((m, d), F32),
        compiler_params=_cparams(("parallel",)),
        name="xattn_final",
    )(h, gx.reshape(1, d), wq, kv, wo, gf.reshape(1, d))


def kernel(x, positions, mem, w_in, b_gate, norm1_g, lam_q1, lam_k1, lam_q2, lam_k2,
           subln_g, rel_bias, sg_ln_g, sg_ln_b, sg_w, sg_b, w_proj_a, w_proj_b, w_out,
           norm_x_g, norm_mem_g, w_xq, w_xkv, w_xo, final_g):
    bsz, s_len, d = x.shape
    assert bsz == 1 and positions.shape == (bsz, s_len)
    depth = w_in.shape[0]
    assert depth == 1
    w = DA_HEADS * DA_V_DIM
    h = x.reshape(s_len, d)
    rel_bias_t = rel_bias.T
    for l in range(depth):
        lam_init = 0.8 - 0.6 * math.exp(-0.3 * l)
        hn = _rmsnorm(h, norm1_g[l], BF16)
        w_in_l = w_in[l]

        def in_proj(col0, ncols, epilogue, out_dtype, extras=()):
            return _wproj([hn], [w_in_l], [col0], ncols, epilogue, out_dtype, extras)

        qkv = in_proj(0, 3 * w, "qkv", BF16)
        zs_a = in_proj(3 * w, w, "silu", F32)
        gv = in_proj(5 * w, w, "gelu", F32)
        guz = _wproj([hn], [w_in_l, w_in_l], [4 * w, 6 * w], w, "gelu_silu_product", F32,
                     tn=512)
        tn = 1024
        gates = in_proj(7 * w, 2 * d, "sigmoid_bias", F32, extras=[
            (b_gate[l].reshape(1, 2 * d), (1, tn), lambda j, i: (0, j))])
        lam_vecs = jnp.stack([lam_q1[l], lam_k1[l], lam_q2[l], lam_k2[l]]).astype(F32)
        a_in = _diff_attention(qkv, zs_a, rel_bias_t, lam_vecs, subln_g[l], lam_init)
        b_in = _spatial_gate(gv, guz, sg_ln_g[l], sg_ln_b[l],
                             sg_w[l].astype(BF16), sg_b[l].T)
        tm, tn = 512, 512
        nb = d // tn
        merged = _wproj([a_in, b_in], [w_proj_a[l], w_proj_b[l]], [0, 0], d, "gated_sum", BF16,
                        extras=[(gates, (tm, tn), lambda j, i: (i, j)),
                                (gates, (tm, tn), lambda j, i, nb=nb: (i, j + nb))],
                        tm=tm, tn=tn)
        tm, tn = 1024, 512
        h = _wproj([merged], [w_out[l]], [0], d, "residual", F32,
                   extras=[(h, (tm, tn), lambda j, i: (i, j))], tm=tm, tn=tn)
        kv = _mem_kv(mem.reshape(-1, d), norm_mem_g[l], w_xkv[l].astype(BF16))
        h = _xattn_final(h, norm_x_g[l], w_xq[l].astype(BF16), kv, w_xo[l].astype(BF16), final_g)
    return h.reshape(bsz, s_len, d)
```

```python
import functools
import math

import jax
import jax.numpy as jnp
from jax import lax
from jax.experimental import pallas as pl
from jax.experimental.pallas import tpu as pltpu

F32 = jnp.float32
BF16 = jnp.bfloat16

EPS = 1e-6
LANE = 128
VMEM_LIMIT = 56 * 1024 * 1024

DA_HEADS = 16
DA_HEAD_DIM = 128
DA_V_DIM = 2 * DA_HEAD_DIM
SG_GROUPS = 16
CHUNK = 128
NUM_BUCKETS = 32
MAX_DISTANCE = 128
XA_HEADS = 4
XA_HEAD_DIM = 128

ATT_T = 512
SOFTMAX_C2 = DA_HEAD_DIM ** -0.5 * math.log2(math.e)
NEG = -1e30


def _cparams(sem):
    return pltpu.CompilerParams(dimension_semantics=sem, vmem_limit_bytes=VMEM_LIMIT)


def _sigmoid(x):
    return 0.5 * jnp.tanh(0.5 * x) + 0.5


def _silu(x):
    h = 0.5 * x
    return h * jnp.tanh(h) + h


def _gelu_tanh(x):
    c = math.sqrt(2.0 / math.pi)
    h = 0.5 * x
    return h * jnp.tanh(x * (c + (c * 0.044715) * (x * x))) + h


def _rmsnorm_kernel(x_ref, g_ref, o_ref):
    x = x_ref[...]
    ms = jnp.mean(x * x, axis=-1, keepdims=True)
    o_ref[...] = ((x * lax.rsqrt(ms + EPS)) * g_ref[...]).astype(o_ref.dtype)


def _rmsnorm(x, g, out_dtype, rows=512):
    m, d = x.shape
    return pl.pallas_call(
        _rmsnorm_kernel,
        grid=(m // rows,),
        in_specs=[pl.BlockSpec((rows, d), lambda i: (i, 0)),
                  pl.BlockSpec((1, d), lambda i: (0, 0))],
        out_specs=pl.BlockSpec((rows, d), lambda i: (i, 0)),
        out_shape=jax.ShapeDtypeStruct((m, d), out_dtype),
        compiler_params=_cparams(("parallel",)),
        name="rmsnorm",
    )(x, g.reshape(1, d))


def _epilogue(kind, ys, extras, col_tile):
    if kind == "cast":
        return ys[0]
    if kind == "qkv":
        q_tiles = DA_HEADS * DA_V_DIM // ys[0].shape[1]
        return ys[0] * jnp.where(col_tile < q_tiles, SOFTMAX_C2, 1.0)
    if kind == "silu":
        return _silu(ys[0])
    if kind == "gelu":
        return _gelu_tanh(ys[0])
    if kind == "sigmoid_bias":
        return _sigmoid(ys[0] + extras[0][...])
    if kind == "residual":
        return extras[0][...] + ys[0]
    if kind == "gated_sum":
        return extras[0][...] * ys[0] + extras[1][...] * ys[1]
    if kind == "gelu_silu_product":
        return _gelu_tanh(ys[0]) * _silu(ys[1])
    raise ValueError(kind)


def _wproj_kernel(*refs, n_a, n_w, n_extra, epilogue, kc, col_blk0):
    a_refs = refs[:n_a]
    refs = refs[n_a:]
    w_refs = refs[:n_w]
    extras = refs[n_w:n_w + n_extra]
    o_ref = refs[n_w + n_extra]
    wb_refs = refs[n_w + n_extra + 1:2 * n_w + n_extra + 1]
    st_refs = refs[2 * n_w + n_extra + 1:3 * n_w + n_extra + 1]
    sem = refs[-1]
    j = pl.program_id(0)
    i = pl.program_id(1)
    nj = pl.num_programs(0)
    ni = pl.num_programs(1)
    tn = o_ref.shape[1]

    def chunk_copy(w, jt, c, slot):
        return pltpu.make_async_copy(
            w_refs[w].at[pl.ds(c * kc, kc), pl.ds((col_blk0[w] + jt) * tn, tn)],
            st_refs[w].at[slot], sem.at[w, slot])

    def cast_chunk(w, c, slot, wslot):
        row0 = c * kc if isinstance(c, int) else pl.multiple_of(c * kc, kc)
        wb_refs[w][wslot, pl.ds(row0, kc), :] = st_refs[w][slot].astype(BF16)

    n_chunks = w_refs[0].shape[0] // kc

    @pl.when((j == 0) & (i == 0))
    def _first_tile():
        for w in range(n_w):
            chunk_copy(w, 0, 0, 0).start()
        for c in range(n_chunks):
            for w in range(n_w):
                if c + 1 < n_chunks:
                    chunk_copy(w, 0, c + 1, (c + 1) % 2).start()
                chunk_copy(w, 0, c, c % 2).wait()
                cast_chunk(w, c, c % 2, 0)
        for w in range(n_w):
            chunk_copy(w, 1 % nj, 0, 0).start()

    slot = i % 2
    cur = j % 2
    nxt = 1 - cur
    next_tile = (j + 1) % nj

    @pl.when(jnp.logical_not((j == nj - 1) & (i == ni - 1)))
    def _prefetch():
        wrap = i == ni - 1
        jt = jnp.where(wrap, (j + 2) % nj, next_tile)
        c = jnp.where(wrap, 0, i + 1)
        for w in range(n_w):
            chunk_copy(w, jt, c, 1 - slot).start()

    for w in range(n_w):
        chunk_copy(w, next_tile, i, slot).wait()
        cast_chunk(w, i, slot, nxt)

    ys = [jnp.dot(a_refs[w % n_a][...], wb_refs[w][cur], preferred_element_type=F32)
          for w in range(n_w)]
    o_ref[...] = _epilogue(epilogue, ys, extras, col_blk0[0] + j).astype(o_ref.dtype)


def _wproj(a_list, w_list, col0s, ncols, epilogue, out_dtype, extras=(), tm=1024, tn=1024):
    n_a, n_w = len(a_list), len(w_list)
    m, k = a_list[0].shape
    ni = m // tm
    assert n_a in (1, n_w) and len(col0s) == n_w and all(c % tn == 0 for c in col0s)
    assert ncols % tn == 0 and m % tm == 0 and ni % 2 == 0
    assert ncols // tn >= 2 and k % ni == 0
    kc = k // ni
    in_specs = ([pl.BlockSpec((tm, k), lambda j, i: (i, 0))] * n_a
                + [pl.BlockSpec(memory_space=pl.ANY)] * n_w
                + [pl.BlockSpec(bs, im) for _, bs, im in extras])
    return pl.pallas_call(
        functools.partial(_wproj_kernel, n_a=n_a, n_w=n_w, n_extra=len(extras),
                          epilogue=epilogue, kc=kc, col_blk0=tuple(c // tn for c in col0s)),
        grid=(ncols // tn, ni),
        in_specs=in_specs,
        out_specs=pl.BlockSpec((tm, tn), lambda j, i: (i, j)),
        out_shape=jax.ShapeDtypeStruct((m, ncols), out_dtype),
        scratch_shapes=([pltpu.VMEM((2, k, tn), BF16)] * n_w
                        + [pltpu.VMEM((2, kc, tn), F32)] * n_w
                        + [pltpu.SemaphoreType.DMA((n_w, 2))]),
        compiler_params=_cparams(("arbitrary", "arbitrary")),
        name="proj_" + epilogue,
    )(*a_list, *w_list, *[e[0] for e in extras])


def _t5_bucket(n):
    max_exact = NUM_BUCKETS // 2
    nf = jnp.maximum(n, 1).astype(F32)
    large = max_exact + (jnp.log(nf / max_exact) / math.log(MAX_DISTANCE / max_exact)
                         * (NUM_BUCKETS - max_exact)).astype(jnp.int32)
    large = jnp.minimum(large, NUM_BUCKETS - 1)
    return jnp.where(n < max_exact, n, large)


def _attn_kernel(rb_ref, q_ref, k_ref, v_ref, zs_ref, lam_ref, g_ref, o_ref,
                 dn_ref, corner_ref, acc_ref, m_ref, l_ref,
                 *, lam_init):
    t = ATT_T
    dh = DA_HEAD_DIM
    h = pl.program_id(0)
    i = pl.program_id(1)

    @pl.when(i == 0)
    def _build_bias():
        r = lax.broadcasted_iota(jnp.int32, (LANE, LANE), 0)
        c = lax.broadcasted_iota(jnp.int32, (LANE, LANE), 1)
        n0 = r - c
        b0 = _t5_bucket(jnp.maximum(n0, 0))
        b1 = _t5_bucket(n0 + LANE)
        last = rb_ref[h, NUM_BUCKETS - 1]
        p0 = jnp.zeros((LANE, LANE), F32)
        p1 = jnp.zeros((LANE, LANE), F32)
        for b in range(NUM_BUCKETS):
            val = (rb_ref[h, b] - last) * math.log2(math.e)
            p0 = jnp.where(b0 == b, val, p0)
            p1 = jnp.where(b1 == b, val, p1)
        p0 = jnp.where(n0 >= 0, p0, NEG)
        nblk = t // LANE
        zero = jnp.zeros((LANE, LANE), F32)
        neg = jnp.full((LANE, LANE), NEG, F32)
        for a in range(nblk):
            for b in range(nblk):
                blk = p0 if a == b else p1 if a == b + 1 else zero if a > b else neg
                dn_ref[a * LANE:(a + 1) * LANE, b * LANE:(b + 1) * LANE] = blk
        corner_ref[...] = p1

    def step(kb, width, near=False, first=False):
        tk = width * t
        off = pl.multiple_of(kb * t, t)
        k = k_ref[pl.ds(off, tk), :]
        v = v_ref[pl.ds(off, tk), :]
        ss = []
        for c in range(2):
            q = q_ref[:, c * dh:(c + 1) * dh]
            s = lax.dot_general(q, k[:, c * dh:(c + 1) * dh], (((1,), (1,)), ((), ())),
                                preferred_element_type=F32)
            pieces = []
            for j in range(tk // LANE):
                piece = s[:, j * LANE:(j + 1) * LANE]
                tile, col = divmod(j * LANE, t)
                if near and tile == width - 1:
                    piece = piece + dn_ref[:, col:col + LANE]
                elif near and tile == width - 2 and col == t - LANE:
                    piece = jnp.concatenate(
                        [piece[:LANE] + corner_ref[...], piece[LANE:]], axis=0)
                pieces.append(piece)
            ss.append(pieces)
        m_out, l_out, alphas, probs = [], [], [], []
        for c in range(2):
            rows = slice(c * t, (c + 1) * t)
            m_cur = functools.reduce(jnp.maximum, ss[c])
            m_row = jnp.max(m_cur, axis=1, keepdims=True)
            if first:
                m_new = jnp.broadcast_to(m_row, (t, LANE))
                alpha = None
            else:
                m_old = m_ref[rows, :]
                m_new = jnp.maximum(m_old, m_row)
                alpha = jnp.exp2(m_old - m_new)
            ps = [jnp.exp2(piece - m_new) for piece in ss[c]]
            l_new = functools.reduce(lambda a, b: a + b, ps)
            l_out.append(l_new if first else alpha * l_ref[rows, :] + l_new)
            probs.append(jnp.concatenate(ps, axis=1).astype(BF16))
            alphas.append(alpha)
            m_out.append(m_new)
        acc_out = []
        for c in range(2):
            rows = slice(c * t, (c + 1) * t)
            pv = jnp.dot(probs[c], v, preferred_element_type=F32)
            if first:
                acc_out.append(pv)
            else:
                alpha_w = jnp.concatenate([alphas[c]] * (acc_ref.shape[1] // LANE), axis=1)
                acc_out.append(alpha_w * acc_ref[rows, :] + pv)
        m_ref[...] = jnp.concatenate(m_out, axis=0)
        l_ref[...] = jnp.concatenate(l_out, axis=0)
        acc_ref[...] = jnp.concatenate(acc_out, axis=0)

    n_far = jnp.maximum(i - 1, 0)
    n_hex = n_far // 6
    left = n_far - 6 * n_hex
    has_quad = left >= 4
    rem = jnp.where(has_quad, left - 4, left)

    @pl.when(n_hex >= 1)
    def _first_hex():
        step(0, 6, first=True)

    def far_hex(j, carry):
        step(6 * j, 6)
        return carry

    lax.fori_loop(1, n_hex, far_hex, 0)

    for first in (True, False):
        @pl.when(has_quad & ((n_hex == 0) == first))
        def _far_quad(first=first):
            step(6 * n_hex, 4, first=first)

    for tail_rem in range(4):
        for first in (True, False):
            @pl.when((i >= 1) & (rem == tail_rem) & ((n_far < 4) == first))
            def _tail(tail_rem=tail_rem, first=first):
                step(n_far - tail_rem, tail_rem + 2, near=True, first=first)

    @pl.when(i == 0)
    def _diagonal_only():
        step(0, 1, near=True, first=True)

    lam_v = lam_ref[...]
    lam = (jnp.exp(jnp.sum(lam_v[0:1] * lam_v[1:2], axis=1, keepdims=True))
           - jnp.exp(jnp.sum(lam_v[2:3] * lam_v[3:4], axis=1, keepdims=True)) + lam_init)
    l_sum = jnp.sum(l_ref[...], axis=1, keepdims=True)
    o = acc_ref[0:t, :] / l_sum[0:t] - lam * (acc_ref[t:2 * t, :] / l_sum[t:2 * t])
    ms = jnp.mean(o * o, axis=-1, keepdims=True)
    y = ((o * lax.rsqrt(ms + EPS)) * g_ref[...]) * (1.0 - lam_init)
    o_ref[...] = (y * zs_ref[...]).astype(o_ref.dtype)


def _diff_attention(qkv, zs, rel_bias_t, lam_vecs, subln_g, lam_init):
    s_len = qkv.shape[0]
    t = ATT_T
    hw = DA_V_DIM
    nh = DA_HEADS
    grid = (nh, s_len // t)
    return pl.pallas_call(
        functools.partial(_attn_kernel, lam_init=lam_init),
        grid=grid,
        in_specs=[
            pl.BlockSpec(memory_space=pltpu.SMEM),
            pl.BlockSpec((t, hw), lambda h, i: (i, h)),
            pl.BlockSpec((s_len, hw), lambda h, i: (0, nh + h)),
            pl.BlockSpec((s_len, hw), lambda h, i: (0, 2 * nh + h)),
            pl.BlockSpec((t, hw), lambda h, i: (i, h)),
            pl.BlockSpec((4, DA_HEAD_DIM), lambda h, i: (0, 0)),
            pl.BlockSpec((1, hw), lambda h, i: (0, 0)),
        ],
        out_specs=pl.BlockSpec((t, hw), lambda h, i: (i, h)),
        out_shape=jax.ShapeDtypeStruct((s_len, nh * hw), BF16),
        scratch_shapes=[
            pltpu.VMEM((t, t), F32), pltpu.VMEM((LANE, LANE), F32),
            pltpu.VMEM((2 * t, hw), F32),
            pltpu.VMEM((2 * t, LANE), F32), pltpu.VMEM((2 * t, LANE), F32),
        ],
        compiler_params=_cparams(("arbitrary", "arbitrary")),
        name="diff_attention",
    )(rel_bias_t, qkv, qkv, qkv, zs, lam_vecs, subln_g.reshape(1, hw))


def _sgu_kernel(gv_ref, guz_ref, lng_ref, lnb_ref, w_ref, bt_ref, o_ref):
    r = lax.broadcasted_iota(jnp.int32, (CHUNK, CHUNK), 0)
    c = lax.broadcasted_iota(jnp.int32, (CHUNK, CHUNK), 1)
    tril = r >= c
    gd = gv_ref.shape[1] // SG_GROUPS
    for ch in range(gv_ref.shape[0] // CHUNK):
        rows = slice(ch * CHUNK, (ch + 1) * CHUNK)
        v = gv_ref[rows, :]
        mu = jnp.mean(v, axis=-1, keepdims=True)
        xc = v - mu
        var = jnp.mean(xc * xc, axis=-1, keepdims=True)
        vn = ((xc * lax.rsqrt(var + EPS)) * lng_ref[...] + lnb_ref[...]).astype(BF16)
        for g in range(SG_GROUPS):
            sl = slice(g * gd, (g + 1) * gd)
            wg = jnp.where(tril, w_ref[g], jnp.zeros((CHUNK, CHUNK), BF16))
            mixed = jnp.dot(wg, vn[:, sl], preferred_element_type=F32) + bt_ref[:, g:g + 1]
            o_ref[rows, sl] = (guz_ref[rows, sl] * mixed).astype(o_ref.dtype)


def _spatial_gate(gv, guz, ln_g, ln_b, w_bf16, b_t, chunks_per_step=2):
    s_len, width = gv.shape
    rows = chunks_per_step * CHUNK
    row = pl.BlockSpec((rows, width), lambda i: (i, 0))
    vec = pl.BlockSpec((1, width), lambda i: (0, 0))
    return pl.pallas_call(
        _sgu_kernel,
        grid=(s_len // rows,),
        in_specs=[row, row, vec, vec,
                  pl.BlockSpec((SG_GROUPS, CHUNK, CHUNK), lambda i: (0, 0, 0)),
                  pl.BlockSpec((CHUNK, SG_GROUPS), lambda i: (0, 0))],
        out_specs=row,
        out_shape=jax.ShapeDtypeStruct((s_len, width), BF16),
        compiler_params=_cparams(("parallel",)),
        name="spatial_gate",
    )(gv, guz, ln_g.reshape(1, width), ln_b.reshape(1, width), w_bf16, b_t)


def _mem_kv_kernel(mem_ref, g_ref, w_ref, o_ref):
    x = mem_ref[...]
    ms = jnp.mean(x * x, axis=-1, keepdims=True)
    mn = ((x * lax.rsqrt(ms + EPS)) * g_ref[...]).astype(BF16)
    o_ref[...] = jnp.dot(mn, w_ref[...], preferred_element_type=F32).astype(o_ref.dtype)


def _mem_kv(mem, g, w_xkv):
    ml, d = mem.shape
    n = w_xkv.shape[1]
    return pl.pallas_call(
        _mem_kv_kernel,
        grid=(1,),
        in_specs=[pl.BlockSpec((ml, d), lambda i: (0, 0)),
                  pl.BlockSpec((1, d), lambda i: (0, 0)),
                  pl.BlockSpec((d, n), lambda i: (0, 0))],
        out_specs=pl.BlockSpec((ml, n), lambda i: (0, 0)),
        out_shape=jax.ShapeDtypeStruct((ml, n), BF16),
        compiler_params=_cparams(("arbitrary",)),
        name="mem_kv",
    )(mem, g.reshape(1, d), w_xkv)


def _xattn_kernel(h_ref, gx_ref, wq_ref, mem_ref, gm_ref, wkv_ref, wo_ref, gf_ref, o_ref,
                  kv_ref):
    hd = XA_HEAD_DIM
    xw = XA_HEADS * hd

    @pl.when(pl.program_id(0) == 0)
    def _project_memory():
        _mem_kv_kernel(mem_ref, gm_ref, wkv_ref, kv_ref)

    half = h_ref.shape[0] // 2
    parts = [slice(0, half), slice(half, 2 * half)]
    hs = [h_ref[r, :] for r in parts]
    hns = []
    for h in hs:
        ms = jnp.mean(h * h, axis=-1, keepdims=True)
        hns.append(((h * lax.rsqrt(ms + EPS)) * gx_ref[...]).astype(BF16))
    qs = [jnp.dot(hn, wq_ref[...], preferred_element_type=F32).astype(BF16) for hn in hns]
    os_ = []
    for q in qs:
        outs = []
        for a in range(XA_HEADS):
            k = kv_ref[:, a * hd:(a + 1) * hd]
            v = kv_ref[:, xw + a * hd:xw + (a + 1) * hd]
            s = lax.dot_general(q[:, a * hd:(a + 1) * hd], k, (((1,), (1,)), ((), ())),
                                preferred_element_type=F32) * (hd ** -0.5)
            p = jnp.exp(s - jnp.max(s, axis=-1, keepdims=True))
            p = p / jnp.sum(p, axis=-1, keepdims=True)
            outs.append(jnp.dot(p.astype(BF16), v, preferred_element_type=F32))
        os_.append(jnp.concatenate(outs, axis=-1).astype(BF16))
    h2s = [h + jnp.dot(o, wo_ref[...], preferred_element_type=F32) for h, o in zip(hs, os_)]
    outs = []
    for h2 in h2s:
        ms2 = jnp.mean(h2 * h2, axis=-1, keepdims=True)
        outs.append((h2 * lax.rsqrt(ms2 + EPS)) * gf_ref[...])
    o_ref[...] = jnp.concatenate(outs, axis=0)


def _xattn_final(h, gx, wq, mem, gm, wkv, wo, gf, tm=256):
    m, d = h.shape
    row = pl.BlockSpec((tm, d), lambda i: (i, 0))
    vec = pl.BlockSpec((1, d), lambda i: (0, 0))
    full = lambda arr: pl.BlockSpec(arr.shape, lambda i: (0, 0),
                                    pipeline_mode=pl.Buffered(buffer_count=1))
    return pl.pallas_call(
        _xattn_kernel,
        grid=(m // tm,),
        in_specs=[row, vec, full(wq), full(mem), vec, full(wkv), full(wo), vec],
        out_specs=row,
        out_shape=jax.ShapeDtypeStruct((m, d), F32),
        scratch_shapes=[pltpu.VMEM((mem.shape[0], wkv.shape[1]), BF16)],
        compiler_params=_cparams(("arbitrary",)),
        name="xattn_final",
    )(h, gx.reshape(1, d), wq, mem, gm.reshape(1, d), wkv, wo, gf.reshape(1, d))


def kernel(x, positions, mem, w_in, b_gate, norm1_g, lam_q1, lam_k1, lam_q2, lam_k2,
           subln_g, rel_bias, sg_ln_g, sg_ln_b, sg_w, sg_b, w_proj_a, w_proj_b, w_out,
           norm_x_g, norm_mem_g, w_xq, w_xkv, w_xo, final_g):
    bsz, s_len, d = x.shape
    assert bsz == 1 and positions.shape == (bsz, s_len)
    depth = w_in.shape[0]
    assert depth == 1
    w = DA_HEADS * DA_V_DIM
    h = x.reshape(s_len, d)
    rel_bias_t = rel_bias.T
    for l in range(depth):
        lam_init = 0.8 - 0.6 * math.exp(-0.3 * l)
        hn = _rmsnorm(h, norm1_g[l], BF16)
        w_in_l = w_in[l]

        def in_proj(col0, ncols, epilogue, out_dtype, extras=()):
            return _wproj([hn], [w_in_l], [col0], ncols, epilogue, out_dtype, extras)

        qkv = in_proj(0, 3 * w, "qkv", BF16)
        zs_a = in_proj(3 * w, w, "silu", F32)
        gv = in_proj(5 * w, w, "gelu", F32)
        guz = _wproj([hn], [w_in_l, w_in_l], [4 * w, 6 * w], w, "gelu_silu_product", F32,
                     tn=512)
        tn = 1024
        gates = in_proj(7 * w, 2 * d, "sigmoid_bias", F32, extras=[
            (b_gate[l].reshape(1, 2 * d), (1, tn), lambda j, i: (0, j))])
        lam_vecs = jnp.stack([lam_q1[l], lam_k1[l], lam_q2[l], lam_k2[l]]).astype(F32)
        a_in = _diff_attention(qkv, zs_a, rel_bias_t, lam_vecs, subln_g[l], lam_init)
        b_in = _spatial_gate(gv, guz, sg_ln_g[l], sg_ln_b[l],
                             sg_w[l].astype(BF16), sg_b[l].T)
        tm, tn = 512, 512
        nb = d // tn
        merged = _wproj([a_in, b_in], [w_proj_a[l], w_proj_b[l]], [0, 0], d, "gated_sum", BF16,
                        extras=[(gates, (tm, tn), lambda j, i: (i, j)),
                                (gates, (tm, tn), lambda j, i, nb=nb: (i, j + nb))],
                        tm=tm, tn=tn)
        tm, tn = 1024, 512
        h = _wproj([merged], [w_out[l]], [0], d, "residual", F32,
                   extras=[(h, (tm, tn), lambda j, i: (i, j))], tm=tm, tn=tn)
        h = _xattn_final(h, norm_x_g[l], w_xq[l].astype(BF16), mem.reshape(-1, d),
                         norm_mem_g[l], w_xkv[l].astype(BF16), w_xo[l].astype(BF16), final_g)
    return h.reshape(bsz, s_len, d)
```
